```python
import math
import jax, jax.numpy as jnp
from jax import lax
import numpy as np

D_MODEL = 1024
BATCH = 4
SEQ = 4096
DEPTH = 1
DEC_BATCH = 2
DEC_SEQ = 8192
PAST_LEN = 128

D_RNN = 1024
LRU_BLOCKS = 4
LRU_BW = D_RNN // LRU_BLOCKS
LRU_C = 8.0
CONV_W = 4
CONV_PAD_L = (CONV_W - 1) // 2
CONV_PAD_R = CONV_W - 1 - CONV_PAD_L
N_HEADS = 8
HEAD_DIM = 64
V_DIM = 2 * HEAD_DIM
ATTN_W = N_HEADS * 2 * HEAD_DIM
Q_BLOCK = 128
NUM_BUCKETS = 32
MAX_DISTANCE = 128
PEER_HEADS = 8
N_KEYS = 128
N_EXPERTS = N_KEYS * N_KEYS
PEER_TOPK = 16
D_KEY = 256
D_HALF = D_KEY // 2
PEER_CHUNK = 128
IN_W = 2 * D_RNN + 3 * ATTN_W + 2 * D_MODEL
EPS = 1e-6

kernel_name = 'hybrid_rglru_diffattn_peer_encoder'


def rmsnorm(x, g):
    xf = x.astype(jnp.float32)
    y = xf * lax.rsqrt(jnp.mean(xf * xf, axis=-1, keepdims=True) + EPS)
    return (y * g.astype(jnp.float32)).astype(x.dtype)


def rel_bucket(rel):
    nb = NUM_BUCKETS // 2
    ret = jnp.where(rel > 0, nb, 0).astype(jnp.int32)
    n = jnp.abs(rel)
    max_exact = nb // 2
    nf = jnp.maximum(n, 1).astype(jnp.float32)
    large = max_exact + (jnp.log(nf / max_exact) / math.log(MAX_DISTANCE / max_exact) * (nb - max_exact)).astype(jnp.int32)
    large = jnp.minimum(large, nb - 1)
    return ret + jnp.where(n < max_exact, n, large)


def centred_conv(x, w, b):
    S = x.shape[1]
    xp = jnp.pad(x, ((0, 0), (CONV_PAD_L, CONV_PAD_R), (0, 0)))
    y = b + xp[:, 0:S] * w[0]
    for t in range(1, CONV_W):
        y = y + xp[:, t:t + S] * w[t]
    return y


def _lin_combine(e1, e2):
    a1, b1 = e1
    a2, b2 = e2
    return a1 * a2, a2 * b1 + b2


def rglru_dir(x, w_a, b_a, w_x, b_x, lam, reverse):
    B, S, _ = x.shape
    xb = x.reshape(B, S, LRU_BLOCKS, LRU_BW)
    r = jax.nn.sigmoid((jnp.einsum('bsnc,ncd->bsnd', xb, w_a) + b_a).astype(jnp.float32)).reshape(B, S, D_RNN)
    i = jax.nn.sigmoid((jnp.einsum('bsnc,ncd->bsnd', xb, w_x) + b_x).astype(jnp.float32)).reshape(B, S, D_RNN)
    log_a = -LRU_C * r * jax.nn.softplus(-lam.astype(jnp.float32))
    a = jnp.exp(log_a)
    mult = jnp.sqrt(-jnp.expm1(2.0 * log_a))
    boundary = S - 1 if reverse else 0
    mult = mult.at[:, boundary].set(1.0)
    bx = mult * i * x.astype(jnp.float32)
    _, h = lax.associative_scan(_lin_combine, (a, bx), reverse=reverse, axis=1)
    return h


def diff_attention(q, k, v, lam, lam_init, subln_g, rel_bias):
    B, S = q.shape[0], q.shape[1]
    nblk = S // Q_BLOCK
    qb = q.reshape(B, nblk, Q_BLOCK, N_HEADS, 2, HEAD_DIM).transpose(1, 0, 2, 3, 4, 5)
    starts = jnp.arange(nblk, dtype=jnp.int32) * Q_BLOCK
    kpos = jnp.arange(S, dtype=jnp.int32)
    scale = HEAD_DIM ** -0.5

    def block(args):
        qblk, s0 = args
        qpos = s0 + jnp.arange(Q_BLOCK, dtype=jnp.int32)
        bucket = rel_bucket(kpos[None, :] - qpos[:, None])
        bias = jnp.transpose(rel_bias[bucket], (2, 0, 1)).astype(jnp.float32)
        logits = jnp.einsum('bqhcd,bkhcd->bhcqk', qblk, k).astype(jnp.float32) * scale + bias[None, :, None]
        p = jax.nn.softmax(logits, axis=-1)
        wgt = p[:, :, 0] - lam * p[:, :, 1]
        o = jnp.einsum('bhqk,bkhe->bqhe', wgt.astype(v.dtype), v)
        return rmsnorm(o, subln_g) * (1.0 - lam_init)

    out = lax.map(block, (qb, starts))
    return out.transpose(1, 0, 2, 3, 4).reshape(B, S, ATTN_W)


def peer(x, wq, keys, pu, pv):
    B, S, D = x.shape
    T = B * S
    xc = x.reshape(T // PEER_CHUNK, PEER_CHUNK, D)

    def chunk(xt):
        q = (xt @ wq).reshape(PEER_CHUNK, PEER_HEADS, 2, D_HALF)
        s = jnp.einsum('chpd,hpnd->chpn', q, keys).astype(jnp.float32)
        sv, si = lax.top_k(s, PEER_TOPK)
        cand_s = (sv[:, :, 0, :, None] + sv[:, :, 1, None, :]).reshape(PEER_CHUNK, PEER_HEADS, PEER_TOPK * PEER_TOPK)
        cand_i = (si[:, :, 0, :, None] * N_KEYS + si[:, :, 1, None, :]).reshape(PEER_CHUNK, PEER_HEADS, PEER_TOPK * PEER_TOPK)
        top_s, pos = lax.top_k(cand_s, PEER_TOPK)
        idx = jnp.take_along_axis(cand_i, pos, axis=-1)
        g = jax.nn.softmax(top_s, axis=-1)
        u = pu[idx]
        act = jax.nn.gelu(jnp.einsum('chkd,cd->chk', u, xt).astype(jnp.float32))
        return jnp.einsum('chk,chkd->cd', (g * act).astype(xt.dtype), pv[idx])

    return lax.map(chunk, xc).reshape(B, S, D)


def temporal_mix(u, w_in, b_gate, conv_w, conv_b, lru_wa_f, lru_ba_f, lru_wx_f, lru_bx_f, lru_lam_f,
                 lru_wa_b, lru_ba_b, lru_wx_b, lru_bx_b, lru_lam_b, lam, lam_init, subln_g, rel_bias,
                 w_rnn_out, w_attn_out, w_out):
    B, S, _ = u.shape
    proj = u @ w_in
    o1 = D_RNN
    o2 = 2 * D_RNN
    o3 = o2 + ATTN_W
    o4 = o3 + ATTN_W
    o5 = o4 + ATTN_W
    o6 = o5 + D_MODEL
    x_rnn, g_rnn, q, k, v, gl_r, gl_a = jnp.split(proj, [o1, o2, o3, o4, o5, o6], axis=-1)
    xc = centred_conv(x_rnn, conv_w, conv_b)
    h = (rglru_dir(xc, lru_wa_f, lru_ba_f, lru_wx_f, lru_bx_f, lru_lam_f, False)
         + rglru_dir(xc, lru_wa_b, lru_ba_b, lru_wx_b, lru_bx_b, lru_lam_b, True))
    y_rnn = (h * jax.nn.gelu(g_rnn.astype(jnp.float32))).astype(u.dtype) @ w_rnn_out
    q = q.reshape(B, S, N_HEADS, 2, HEAD_DIM)
    k = k.reshape(B, S, N_HEADS, 2, HEAD_DIM)
    v = v.reshape(B, S, N_HEADS, V_DIM)
    y_attn = diff_attention(q, k, v, lam, lam_init, subln_g, rel_bias) @ w_attn_out
    b_r, b_a = jnp.split(b_gate, 2)
    g_r = jax.nn.sigmoid((gl_r + b_r).astype(jnp.float32))
    g_a = jax.nn.sigmoid((gl_a + b_a).astype(jnp.float32))
    merged = (g_r * y_rnn.astype(jnp.float32) + g_a * y_attn.astype(jnp.float32)).astype(u.dtype)
    return merged @ w_out


def trunk(x, norm1_g, w_in, b_gate, conv_w, conv_b, lru_wa_f, lru_ba_f, lru_wx_f, lru_bx_f, lru_lam_f,
          lru_wa_b, lru_ba_b, lru_wx_b, lru_bx_b, lru_lam_b, lam_q1, lam_k1, lam_q2, lam_k2, subln_g,
          rel_bias, w_rnn_out, w_attn_out, w_out, norm2_g, peer_wq, peer_keys, peer_u, peer_v, final_g):
    for l in range(DEPTH):
        lam_init = 0.8 - 0.6 * math.exp(-0.3 * l)
        lam = (jnp.exp(jnp.sum(lam_q1[l].astype(jnp.float32) * lam_k1[l].astype(jnp.float32)))
               - jnp.exp(jnp.sum(lam_q2[l].astype(jnp.float32) * lam_k2[l].astype(jnp.float32))) + lam_init)
        u = rmsnorm(x, norm1_g[l])
        x = x + temporal_mix(u, w_in[l], b_gate[l], conv_w[l], conv_b[l],
                             lru_wa_f[l], lru_ba_f[l], lru_wx_f[l], lru_bx_f[l], lru_lam_f[l],
                             lru_wa_b[l], lru_ba_b[l], lru_wx_b[l], lru_bx_b[l], lru_lam_b[l],
                             lam, lam_init, subln_g[l], rel_bias, w_rnn_out[l], w_attn_out[l], w_out[l])
        x = x + peer(rmsnorm(x, norm2_g[l]), peer_wq[l], peer_keys[l], peer_u[l], peer_v[l])
    return rmsnorm(x, final_g)


def setup_inputs(seed: int = 0) -> dict:
    key = jax.random.key(seed)
    ks = jax.random.split(key, 40)
    f32 = jnp.float32

    def nrm(k, shape, scale):
        return jax.random.normal(k, shape, f32) * scale

    def lru_lambda(k):
        a0 = jax.random.uniform(k, (DEPTH, D_RNN), f32, 0.9, 0.999)
        p = a0 ** (1.0 / LRU_C)
        return jnp.log(p) - jnp.log1p(-p)

    return {
        'x_prompt': nrm(ks[0], (BATCH, SEQ, D_MODEL), 1.0),
        'x_sample': nrm(ks[1], (DEC_BATCH, DEC_SEQ, D_MODEL), 1.0),
        'norm1_g': 1.0 + nrm(ks[2], (DEPTH, D_MODEL), 0.02),
        'w_in': nrm(ks[3], (DEPTH, D_MODEL, IN_W), D_MODEL ** -0.5),
        'b_gate': nrm(ks[4], (DEPTH, 2 * D_MODEL), 0.1),
        'conv_w': nrm(ks[5], (DEPTH, CONV_W, D_RNN), CONV_W ** -0.5),
        'conv_b': nrm(ks[6], (DEPTH, D_RNN), 0.02),
        'lru_wa_f': nrm(ks[7], (DEPTH, LRU_BLOCKS, LRU_BW, LRU_BW), LRU_BW ** -0.5),
        'lru_ba_f': nrm(ks[8], (DEPTH, LRU_BLOCKS, LRU_BW), 0.1),
        'lru_wx_f': nrm(ks[9], (DEPTH, LRU_BLOCKS, LRU_BW, LRU_BW), LRU_BW ** -0.5),
        'lru_bx_f': nrm(ks[10], (DEPTH, LRU_BLOCKS, LRU_BW), 0.1),
        'lru_lam_f': lru_lambda(ks[11]),
        'lru_wa_b': nrm(ks[12], (DEPTH, LRU_BLOCKS, LRU_BW, LRU_BW), LRU_BW ** -0.5),
        'lru_ba_b': nrm(ks[13], (DEPTH, LRU_BLOCKS, LRU_BW), 0.1),
        'lru_wx_b': nrm(ks[14], (DEPTH, LRU_BLOCKS, LRU_BW, LRU_BW), LRU_BW ** -0.5),
        'lru_bx_b': nrm(ks[15], (DEPTH, LRU_BLOCKS, LRU_BW), 0.1),
        'lru_lam_b': lru_lambda(ks[16]),
        'lam_q1': nrm(ks[17], (DEPTH, HEAD_DIM), 0.1),
        'lam_k1': nrm(ks[18], (DEPTH, HEAD_DIM), 0.1),
        'lam_q2': nrm(ks[19], (DEPTH, HEAD_DIM), 0.1),
        'lam_k2': nrm(ks[20], (DEPTH, HEAD_DIM), 0.1),
        'subln_g': 1.0 + nrm(ks[21], (DEPTH, V_DIM), 0.02),
        'rel_bias': nrm(ks[22], (NUM_BUCKETS, N_HEADS), 0.5),
        'w_rnn_out': nrm(ks[23], (DEPTH, D_RNN, D_MODEL), D_RNN ** -0.5),
        'w_attn_out': nrm(ks[24], (DEPTH, ATTN_W, D_MODEL), ATTN_W ** -0.5),
        'w_out': nrm(ks[25], (DEPTH, D_MODEL, D_MODEL), D_MODEL ** -0.5),
        'norm2_g': 1.0 + nrm(ks[26], (DEPTH, D_MODEL), 0.02),
        'peer_wq': nrm(ks[27], (DEPTH, D_MODEL, PEER_HEADS * D_KEY), D_MODEL ** -0.5),
        'peer_keys': nrm(ks[28], (DEPTH, PEER_HEADS, 2, N_KEYS, D_HALF), D_HALF ** -0.5),
        'peer_u': nrm(ks[29], (DEPTH, N_EXPERTS, D_MODEL), D_MODEL ** -0.5),
        'peer_v': nrm(ks[30], (DEPTH, N_EXPERTS, D_MODEL), PEER_HEADS ** -0.5),
        'final_g': 1.0 + nrm(ks[31], (D_MODEL,), 0.02),
    }


def reference(x_prompt, x_sample, norm1_g, w_in, b_gate, conv_w, conv_b, lru_wa_f, lru_ba_f, lru_wx_f,
              lru_bx_f, lru_lam_f, lru_wa_b, lru_ba_b, lru_wx_b, lru_bx_b, lru_lam_b, lam_q1, lam_k1,
              lam_q2, lam_k2, subln_g, rel_bias, w_rnn_out, w_attn_out, w_out, norm2_g, peer_wq,
              peer_keys, peer_u, peer_v, final_g):
    y_prompt = trunk(x_prompt, norm1_g, w_in, b_gate, conv_w, conv_b, lru_wa_f, lru_ba_f, lru_wx_f,
                     lru_bx_f, lru_lam_f, lru_wa_b, lru_ba_b, lru_wx_b, lru_bx_b, lru_lam_b, lam_q1,
                     lam_k1, lam_q2, lam_k2, subln_g, rel_bias, w_rnn_out, w_attn_out, w_out, norm2_g,
                     peer_wq, peer_keys, peer_u, peer_v, final_g)
    y_sample = trunk(x_sample, norm1_g, w_in, b_gate, conv_w, conv_b, lru_wa_f, lru_ba_f, lru_wx_f,
                     lru_bx_f, lru_lam_f, lru_wa_b, lru_ba_b, lru_wx_b, lru_bx_b, lru_lam_b, lam_q1,
                     lam_k1, lam_q2, lam_k2, subln_g, rel_bias, w_rnn_out, w_attn_out, w_out, norm2_g,
                     peer_wq, peer_keys, peer_u, peer_v, final_g)
    return (y_prompt, y_sample)
```

```python
import functools
import math

import jax
import jax.numpy as jnp
from jax import lax
from jax.experimental import pallas as pl
from jax.experimental.pallas import tpu as pltpu

F32 = jnp.float32
BF16 = jnp.bfloat16

D_MODEL = 1024
D_RNN = 1024
LRU_BLOCKS = 4
LRU_BW = D_RNN // LRU_BLOCKS
LRU_C = 8.0
CONV_W = 4
N_HEADS = 8
HEAD_DIM = 64
V_DIM = 2 * HEAD_DIM
ATTN_W = N_HEADS * V_DIM
NUM_BUCKETS = 32
MAX_DISTANCE = 128
PEER_HEADS = 8
N_KEYS = 128
N_EXPERTS = N_KEYS * N_KEYS
PEER_TOPK = 16
D_KEY = 256
D_HALF = D_KEY // 2
N_SEL = PEER_HEADS * PEER_TOPK
IN_W = 2 * D_RNN + 3 * ATTN_W + 2 * D_MODEL
EPS = 1e-6
LAM_INIT = 0.8 - 0.6 * math.exp(-0.3 * 0)

COL_XRNN, COL_GRNN, COL_Q, COL_K, COL_V, COL_GLR, COL_GLA = range(7)

SUBLANES = 8
LANES = 128
ROW_WORDS = D_MODEL // 2
ROW_SUB = ROW_WORDS // LANES
VMEM_LIMIT = 56 * 1024 * 1024


def _rms(x, g):
    return x * lax.rsqrt(jnp.mean(x * x, axis=-1, keepdims=True) + EPS) * g


def _inproj_body(x_ref, g_ref, w_ref, o_ref, xn_ref):
    @pl.when(pl.program_id(1) == 0)
    def _():
        xn_ref[...] = _rms(x_ref[...], g_ref[...]).astype(BF16)

    o_ref[...] = jnp.dot(xn_ref[...], w_ref[...], preferred_element_type=F32)


def _inproj(x2d, g, w):
    t = x2d.shape[0]
    tm, tn = 512, 1024
    return pl.pallas_call(
        _inproj_body,
        grid=(t // tm, IN_W // tn),
        in_specs=[
            pl.BlockSpec((tm, D_MODEL), lambda i, j: (i, 0)),
            pl.BlockSpec((1, D_MODEL), lambda i, j: (0, 0)),
            pl.BlockSpec((D_MODEL, tn), lambda i, j: (0, j)),
        ],
        out_specs=pl.BlockSpec((tm, tn), lambda i, j: (i, j)),
        out_shape=jax.ShapeDtypeStruct((t, IN_W), F32),
        scratch_shapes=[pltpu.VMEM((tm, D_MODEL), BF16)],
        compiler_params=pltpu.CompilerParams(dimension_semantics=("parallel", "arbitrary")),
        name="inproj",
    )(x2d, g, w)


def _lru_body(xc_ref, xp_ref, xn_ref, cw_ref, cb_ref, wa_ref, ba_ref, wx_ref, bx_ref, lam_ref,
              h_ref, a_s, b_s, carry, *, reverse, nchunks, tc):
    i = pl.program_id(1)
    c = (nchunks - 1 - i) if reverse else i
    x = xc_ref[...]
    rows = lax.broadcasted_iota(jnp.int32, (tc, D_RNN), 0)
    prev = jnp.where(c > 0, xp_ref[SUBLANES - 1:SUBLANES, :], 0.0)
    nxt0 = jnp.where(c < nchunks - 1, xn_ref[0:1, :], 0.0)
    nxt1 = jnp.where(c < nchunks - 1, xn_ref[1:2, :], 0.0)
    xm1 = jnp.where(rows == 0, prev, pltpu.roll(x, 1, 0))
    xp1 = jnp.where(rows == tc - 1, nxt0, pltpu.roll(x, tc - 1, 0))
    xp2 = jnp.where(rows == tc - 2, nxt0, jnp.where(rows == tc - 1, nxt1, pltpu.roll(x, tc - 2, 0)))
    cw = cw_ref[...]
    xc = cb_ref[...] + xm1 * cw[0:1] + x * cw[1:2] + xp1 * cw[2:3] + xp2 * cw[3:4]

    xcb = xc.astype(BF16)
    r_parts, i_parts = [], []
    for n in range(LRU_BLOCKS):
        xs = xcb[:, n * LRU_BW:(n + 1) * LRU_BW]
        r_parts.append(jnp.dot(xs, wa_ref[n], preferred_element_type=F32))
        i_parts.append(jnp.dot(xs, wx_ref[n], preferred_element_type=F32))
    r = jax.nn.sigmoid(jnp.concatenate(r_parts, axis=1) + ba_ref[...])
    ig = jax.nn.sigmoid(jnp.concatenate(i_parts, axis=1) + bx_ref[...])
    z = -lam_ref[...]
    softplus = jnp.maximum(z, 0.0) + jnp.log1p(jnp.exp(-jnp.abs(z)))
    log_a = -LRU_C * r * softplus
    a = jnp.exp(log_a)
    mult = jnp.sqrt(-jnp.tanh(log_a) * (a * a + 1.0))
    edge_row = tc - 1 if reverse else 0
    edge_chunk = nchunks - 1 if reverse else 0
    mult = jnp.where(rows == jnp.where(c == edge_chunk, edge_row, -1), 1.0, mult)
    a_s[...] = a
    b_s[...] = mult * ig * xc

    row8 = lax.broadcasted_iota(jnp.int32, (SUBLANES, D_RNN), 0)
    ngroups = tc // SUBLANES
    h0 = jnp.where(i == 0, 0.0, carry[...])

    def step(gi, h):
        g = (ngroups - 1 - gi) if reverse else gi
        off = pl.multiple_of(g * SUBLANES, SUBLANES)
        av = a_s[pl.ds(off, SUBLANES), :]
        bv = b_s[pl.ds(off, SUBLANES), :]
        for s in (1, 2, 4):
            if reverse:
                keep = row8 < SUBLANES - s
                shift = SUBLANES - s
            else:
                keep = row8 >= s
                shift = s
            a_sh = jnp.where(keep, pltpu.roll(av, shift, 0), 1.0)
            b_sh = jnp.where(keep, pltpu.roll(bv, shift, 0), 0.0)
            bv = av * b_sh + bv
            av = av * a_sh
        hv = av * h + bv
        h_ref[pl.ds(off, SUBLANES), :] = hv
        last = hv[0:1, :] if reverse else hv[SUBLANES - 1:SUBLANES, :]
        return jnp.broadcast_to(last, (SUBLANES, D_RNN))

    carry[...] = lax.fori_loop(0, ngroups, step, h0)


def _lru(proj, row0, nb, s, cw, cb, wa, ba, wx, bx, lam, reverse):
    tc = 256
    nchunks = s // tc
    t_all = proj.shape[0]
    blk0 = row0 // tc
    last8 = t_all // SUBLANES - 1

    def cidx(i):
        return (nchunks - 1 - i) if reverse else i

    def cur(b, i):
        return (blk0 + b * nchunks + cidx(i), COL_XRNN)

    def prev8(b, i):
        r = (blk0 + b * nchunks + cidx(i)) * (tc // SUBLANES) - 1
        return (jnp.maximum(r, 0), COL_XRNN)

    def next8(b, i):
        r = (blk0 + b * nchunks + cidx(i) + 1) * (tc // SUBLANES)
        return (jnp.minimum(r, last8), COL_XRNN)

    full = lambda shape: pl.BlockSpec(shape, lambda b, i: (0,) * len(shape))
    body = functools.partial(_lru_body, reverse=reverse, nchunks=nchunks, tc=tc)
    return pl.pallas_call(
        body,
        grid=(nb, nchunks),
        in_specs=[
            pl.BlockSpec((tc, D_RNN), cur),
            pl.BlockSpec((SUBLANES, D_RNN), prev8),
            pl.BlockSpec((SUBLANES, D_RNN), next8),
            full((CONV_W, D_RNN)), full((1, D_RNN)),
            full((LRU_BLOCKS, LRU_BW, LRU_BW)), full((1, D_RNN)),
            full((LRU_BLOCKS, LRU_BW, LRU_BW)), full((1, D_RNN)),
            full((1, D_RNN)),
        ],
        out_specs=pl.BlockSpec((tc, D_RNN), lambda b, i: (b * nchunks + cidx(i), 0)),
        out_shape=jax.ShapeDtypeStruct((nb * s, D_RNN), F32),
        scratch_shapes=[pltpu.VMEM((tc, D_RNN), F32), pltpu.VMEM((tc, D_RNN), F32),
                        pltpu.VMEM((SUBLANES, D_RNN), F32)],
        compiler_params=pltpu.CompilerParams(dimension_semantics=("arbitrary", "arbitrary")),
        name="lru_bwd" if reverse else "lru_fwd",
    )(proj, proj, proj, cw, cb, wa, ba, wx, bx, lam)


def _attn_body(far_ref, q_ref, k_ref, v_ref, bt_ref, lamv_ref, g_ref, o_ref,
               q1_s, q2_s, m1_s, l1_s, acc1_s, m2_s, l2_s, acc2_s, *, nkv):
    h = pl.program_id(1)
    i = pl.program_id(2)
    j = pl.program_id(3)

    @pl.when(j == 0)
    def _():
        q = q_ref[...] * (HEAD_DIM ** -0.5)
        lane = lax.broadcasted_iota(jnp.int32, q.shape, 1)
        q1_s[...] = jnp.where(lane < HEAD_DIM, q, 0.0).astype(BF16)
        q2_s[...] = jnp.where(lane >= HEAD_DIM, q, 0.0).astype(BF16)
        for m_s, l_s, acc_s in ((m1_s, l1_s, acc1_s), (m2_s, l2_s, acc2_s)):
            m_s[...] = jnp.full(m_s.shape, -jnp.inf, F32)
            l_s[...] = jnp.zeros(l_s.shape, F32)
            acc_s[...] = jnp.zeros(acc_s.shape, F32)

    k = k_ref[...].astype(BF16)
    v = v_ref[...].astype(BF16)

    def update(bias):
        for q_s, m_s, l_s, acc_s in ((q1_s, m1_s, l1_s, acc1_s), (q2_s, m2_s, l2_s, acc2_s)):
            s = lax.dot_general(q_s[...], k, (((1,), (1,)), ((), ())), preferred_element_type=F32) + bias
            m_old = m_s[...]
            m_new = jnp.maximum(m_old, jnp.max(s, axis=-1, keepdims=True))
            alpha = jnp.exp(m_old - m_new)
            p = jnp.exp(s - m_new)
            l_s[...] = alpha * l_s[...] + jnp.sum(p, axis=-1, keepdims=True)
            acc_s[...] = alpha * acc_s[...] + jnp.dot(p.astype(BF16), v, preferred_element_type=F32)
            m_s[...] = m_new

    near = jnp.abs(j - i) <= 1

    @pl.when(near)
    def _():
        update(bt_ref[0, 0])

    @pl.when(jnp.logical_not(near))
    def _():
        update(jnp.where(j < i, far_ref[0, h], far_ref[1, h]))

    @pl.when(j == nkv - 1)
    def _():
        lv = lamv_ref[...]
        lam = (jnp.exp(jnp.sum(lv[0:1] * lv[1:2], axis=-1, keepdims=True))
               - jnp.exp(jnp.sum(lv[2:3] * lv[3:4], axis=-1, keepdims=True)) + LAM_INIT)
        o = acc1_s[...] / l1_s[...] - lam * (acc2_s[...] / l2_s[...])
        o_ref[...] = _rms(o, g_ref[...]) * (1.0 - LAM_INIT)


def _attention(proj, row0, nb, s, far, btiles, lamv, subln_g):
    tq = btiles.shape[-1]
    nq = s // tq
    blk0 = row0 // tq
    hb = V_DIM // LANES

    def qmap(b, h, i, j):
        return (blk0 + b * nq + i, COL_Q * N_HEADS * hb + h)

    def kmap(b, h, i, j):
        return (blk0 + b * nq + j, COL_K * N_HEADS * hb + h)

    def vmap(b, h, i, j):
        return (blk0 + b * nq + j, COL_V * N_HEADS * hb + h)

    def bmap(b, h, i, j):
        return (h, jnp.clip(j - i, -1, 1) + 1, 0, 0)

    body = functools.partial(_attn_body, nkv=nq)
    stat = pltpu.VMEM((tq, 1), F32)
    return pl.pallas_call(
        body,
        grid=(nb, N_HEADS, nq, nq),
        in_specs=[
            pl.BlockSpec(memory_space=pltpu.SMEM),
            pl.BlockSpec((tq, V_DIM), qmap),
            pl.BlockSpec((tq, V_DIM), kmap),
            pl.BlockSpec((tq, V_DIM), vmap),
            pl.BlockSpec((1, 1, tq, tq), bmap),
            pl.BlockSpec((4, HEAD_DIM), lambda b, h, i, j: (0, 0)),
            pl.BlockSpec((1, V_DIM), lambda b, h, i, j: (0, 0)),
        ],
        out_specs=pl.BlockSpec((tq, V_DIM), lambda b, h, i, j: (b * nq + i, h)),
        out_shape=jax.ShapeDtypeStruct((nb * s, ATTN_W), F32),
        scratch_shapes=[pltpu.VMEM((tq, V_DIM), BF16), pltpu.VMEM((tq, V_DIM), BF16),
                        stat, stat, pltpu.VMEM((tq, V_DIM), F32),
                        stat, stat, pltpu.VMEM((tq, V_DIM), F32)],
        compiler_params=pltpu.CompilerParams(
            dimension_semantics=("parallel", "parallel", "parallel", "arbitrary"),
            vmem_limit_bytes=VMEM_LIMIT),
        name="diff_attn",
    )(far, proj, proj, proj, btiles, lamv, subln_g)


def _merge_body(x_ref, hf_ref, hb_ref, grnn_ref, attn_ref, glr_ref, gla_ref, bg_ref,
                wr_ref, wa_ref, wo_ref, n2_ref, wq_ref, x1_ref, xn2_ref, qp_ref):
    hg = ((hf_ref[...] + hb_ref[...]) * jax.nn.gelu(grnn_ref[...])).astype(BF16)
    y_rnn = jnp.dot(hg, wr_ref[...], preferred_element_type=F32)
    y_attn = jnp.dot(attn_ref[...].astype(BF16), wa_ref[...], preferred_element_type=F32)
    bg = bg_ref[...]
    g_r = jax.nn.sigmoid(glr_ref[...] + bg[0:1])
    g_a = jax.nn.sigmoid(gla_ref[...] + bg[1:2])
    merged = (g_r * y_rnn + g_a * y_attn).astype(BF16)
    x1 = x_ref[...] + jnp.dot(merged, wo_ref[...], preferred_element_type=F32)
    x1_ref[...] = x1
    xn2 = _rms(x1, n2_ref[...])
    xn2_ref[...] = xn2
    qp_ref[...] = jnp.dot(xn2.astype(BF16), wq_ref[...], preferred_element_type=F32)


def _merge(x2d, hf, hb, proj, attn, bg, wr, wa, wo, n2, wq):
    t = x2d.shape[0]
    tm = 256
    row = lambda c: pl.BlockSpec((tm, D_MODEL), lambda i: (i, c))
    full = lambda shape: pl.BlockSpec(shape, lambda i: (0,) * len(shape))
    qw = PEER_HEADS * D_KEY
    return pl.pallas_call(
        _merge_body,
        grid=(t // tm,),
        in_specs=[row(0), row(0), row(0), row(COL_GRNN), row(0), row(COL_GLR), row(COL_GLA),
                  full((2, D_MODEL)), full((D_RNN, D_MODEL)), full((ATTN_W, D_MODEL)),
                  full((D_MODEL, D_MODEL)), full((1, D_MODEL)), full((D_MODEL, qw))],
        out_specs=[row(0), row(0), pl.BlockSpec((tm, qw), lambda i: (i, 0))],
        out_shape=[jax.ShapeDtypeStruct((t, D_MODEL), F32), jax.ShapeDtypeStruct((t, D_MODEL), F32),
                   jax.ShapeDtypeStruct((t, qw), F32)],
        compiler_params=pltpu.CompilerParams(dimension_semantics=("parallel",),
                                             vmem_limit_bytes=VMEM_LIMIT),
        name="merge_proj",
    )(x2d, hf, hb, proj, attn, proj, proj, bg, wr, wa, wo, n2, wq)


def _topk_rows(s, k):
    n = s.shape[0]
    iota = lax.broadcasted_iota(jnp.int32, s.shape, 0).astype(F32)
    slot = lax.broadcasted_iota(jnp.int32, (k, s.shape[1]), 0)
    vals = jnp.zeros((k, s.shape[1]), F32)
    idxs = jnp.zeros((k, s.shape[1]), F32)
    for t in range(k):
        m = jnp.max(s, axis=0, keepdims=True)
        pos = jnp.min(jnp.where(s == m, iota, float(n)), axis=0, keepdims=True)
        vals = jnp.where(slot == t, m, vals)
        idxs = jnp.where(slot == t, pos, idxs)
        s = jnp.where(iota == pos, -jnp.inf, s)
    return vals, idxs


def _router_body(qp_ref, keys_ref, idx_ref, g_ref):
    slot = lax.broadcasted_iota(jnp.int32, (PEER_TOPK, qp_ref.shape[0]), 0)
    for h in range(PEER_HEADS):
        sv, si = [], []
        for p in range(2):
            c0 = (h * 2 + p) * D_HALF
            q = qp_ref[:, c0:c0 + D_HALF].astype(BF16)
            s = lax.dot_general(keys_ref[h, p], q, (((1,), (1,)), ((), ())), preferred_element_type=F32)
            v, ix = _topk_rows(s, PEER_TOPK)
            sv.append(v)
            si.append(ix)
        cand = jnp.concatenate([sv[0][a:a + 1, :] + sv[1] for a in range(PEER_TOPK)], axis=0)
        top_s, pos = _topk_rows(cand, PEER_TOPK)
        pos = pos.astype(jnp.int32)
        pa = pos >> 4
        pb = pos & (PEER_TOPK - 1)
        idx = jnp.zeros(slot.shape, F32)
        for t in range(PEER_TOPK):
            i1 = jnp.sum(jnp.where(slot == pa[t:t + 1, :], si[0], 0.0), axis=0, keepdims=True)
            i2 = jnp.sum(jnp.where(slot == pb[t:t + 1, :], si[1], 0.0), axis=0, keepdims=True)
            idx = jnp.where(slot == t, i1 * N_KEYS + i2, idx)
        idx = idx.astype(jnp.int32)
        e = jnp.exp(top_s - jnp.max(top_s, axis=0, keepdims=True))
        g = e / jnp.sum(e, axis=0, keepdims=True)
        idx_ref[h * PEER_TOPK:(h + 1) * PEER_TOPK, :] = idx
        g_ref[h * PEER_TOPK:(h + 1) * PEER_TOPK, :] = g


def _router(qp, keys):
    t = qp.shape[0]
    c = 256
    qw = PEER_HEADS * D_KEY
    return pl.pallas_call(
        _router_body,
        grid=(t // c,),
        in_specs=[pl.BlockSpec((c, qw), lambda i: (i, 0)),
                  pl.BlockSpec((PEER_HEADS, 2, N_KEYS, D_HALF), lambda i: (0, 0, 0, 0))],
        out_specs=[pl.BlockSpec((N_SEL, c), lambda i: (0, i)), pl.BlockSpec((N_SEL, c), lambda i: (0, i))],
        out_shape=[jax.ShapeDtypeStruct((N_SEL, t), jnp.int32), jax.ShapeDtypeStruct((N_SEL, t), F32)],
        compiler_params=pltpu.CompilerParams(dimension_semantics=("parallel",)),
        name="peer_router",
    )(qp, keys)


def _pack_body(t_ref, o_ref):
    x = t_ref[...]
    b = pltpu.bitcast(x.astype(BF16).astype(F32), jnp.uint32)
    lo = b[:, 0:ROW_SUB, :] >> 16
    hi = b[:, ROW_SUB:2 * ROW_SUB, :] & jnp.uint32(0xFFFF0000)
    o_ref[...] = pltpu.bitcast(hi | lo, jnp.int32)


def _pack_table(tab3):
    n = tab3.shape[0]
    r = 512
    return pl.pallas_call(
        _pack_body,
        grid=(n // r,),
        in_specs=[pl.BlockSpec((r, SUBLANES, LANES), lambda i: (i, 0, 0))],
        out_specs=pl.BlockSpec((r, ROW_SUB, LANES), lambda i: (i, 0, 0)),
        out_shape=jax.ShapeDtypeStruct((n, ROW_SUB, LANES), jnp.int32),
        compiler_params=pltpu.CompilerParams(dimension_semantics=("parallel",)),
        name="peer_pack",
    )(tab3)


def _unpack(word):
    lo = pltpu.bitcast(word << 16, F32)
    hi = pltpu.bitcast(word & jnp.int32(-65536), F32)
    return lo, hi


def _load_table(tab_hbm, tab_s, sem):
    @pl.when(pl.program_id(0) == 0)
    def _():
        cp = pltpu.make_async_copy(tab_hbm, tab_s, sem)
        cp.start()
        cp.wait()


def _peer_u_body(idx_ref, x_ref, g_ref, tab_hbm, w_ref, tab_s, q_s, gsum_s, s_s, sem, *, tb):
    _load_table(tab_hbm, tab_s, sem)

    @pl.when(pl.program_id(0) == 0)
    def _():
        r = lax.broadcasted_iota(jnp.int32, gsum_s.shape, 0)
        c = lax.broadcasted_iota(jnp.int32, gsum_s.shape, 1)
        gsum_s[...] = jnp.where((c >= r * ROW_SUB) & (c < (r + 1) * ROW_SUB), 1.0, 0.0)

    ones = jnp.ones((SUBLANES, LANES), F32)

    def tok(t, carry):
        xv = x_ref[t]
        xlo = xv[0:ROW_SUB]
        xhi = xv[ROW_SUB:2 * ROW_SUB]
        for j in range(N_SEL):
            lo, hi = _unpack(tab_s[idx_ref[t, j]])
            q_s[j * ROW_SUB:(j + 1) * ROW_SUB, :] = lo * xlo + hi * xhi
        y = jnp.dot(gsum_s[...], q_s[...], preferred_element_type=F32, precision=lax.Precision.HIGHEST)
        s_s[t] = lax.dot_general(ones, y, (((1,), (1,)), ((), ())), preferred_element_type=F32,
                                 precision=lax.Precision.HIGHEST)
        return carry

    lax.fori_loop(0, tb, tok, 0)
    w_ref[...] = g_ref[...] * jax.nn.gelu(s_s[:, 0, :])


def _peer_u(idx_t, x3, g_t, tab):
    t = idx_t.shape[0]
    tb = 128
    body = functools.partial(_peer_u_body, tb=tb)
    return pl.pallas_call(
        body,
        grid=(t // tb,),
        in_specs=[pl.BlockSpec((tb, N_SEL), lambda i: (i, 0), memory_space=pltpu.SMEM),
                  pl.BlockSpec((tb, SUBLANES, LANES), lambda i: (i, 0, 0)),
                  pl.BlockSpec((tb, N_SEL), lambda i: (i, 0)),
                  pl.BlockSpec(memory_space=pl.ANY)],
        out_specs=pl.BlockSpec((tb, N_SEL), lambda i: (i, 0)),
        out_shape=jax.ShapeDtypeStruct((t, N_SEL), F32),
        scratch_shapes=[pltpu.VMEM((N_EXPERTS, ROW_SUB, LANES), jnp.int32),
                        pltpu.VMEM((N_SEL * ROW_SUB, LANES), F32),
                        pltpu.VMEM((N_SEL, N_SEL * ROW_SUB), F32),
                        pltpu.VMEM((tb, SUBLANES, LANES), F32),
                        pltpu.SemaphoreType.DMA(())],
        compiler_params=pltpu.CompilerParams(dimension_semantics=("arbitrary",),
                                             vmem_limit_bytes=VMEM_LIMIT),
        name="peer_u",
    )(idx_t, x3, g_t, tab)


def _peer_v_body(idx_ref, w_ref, tab_hbm, o_ref, tab_s, sem, *, tb):
    _load_table(tab_hbm, tab_s, sem)
    nacc = 4

    def tok(t, carry):
        acc = [None] * (2 * nacc)
        for j in range(N_SEL):
            lo, hi = _unpack(tab_s[idx_ref[t, j]])
            wj = w_ref[t, j]
            a = j % nacc
            acc[2 * a] = wj * lo if acc[2 * a] is None else acc[2 * a] + wj * lo
            acc[2 * a + 1] = wj * hi if acc[2 * a + 1] is None else acc[2 * a + 1] + wj * hi
        lo = (acc[0] + acc[2]) + (acc[4] + acc[6])
        hi = (acc[1] + acc[3]) + (acc[5] + acc[7])
        o_ref[t] = jnp.concatenate([lo, hi], axis=0)
        return carry

    lax.fori_loop(0, tb, tok, 0)


def _peer_v(idx_t, w, tab):
    t = idx_t.shape[0]
    tb = 128
    body = functools.partial(_peer_v_body, tb=tb)
    return pl.pallas_call(
        body,
        grid=(t // tb,),
        in_specs=[pl.BlockSpec((tb, N_SEL), lambda i: (i, 0), memory_space=pltpu.SMEM),
                  pl.BlockSpec((tb, N_SEL), lambda i: (i, 0), memory_space=pltpu.SMEM),
                  pl.BlockSpec(memory_space=pl.ANY)],
        out_specs=pl.BlockSpec((tb, SUBLANES, LANES), lambda i: (i, 0, 0)),
        out_shape=jax.ShapeDtypeStruct((t, SUBLANES, LANES), F32),
        scratch_shapes=[pltpu.VMEM((N_EXPERTS, ROW_SUB, LANES), jnp.int32),
                        pltpu.SemaphoreType.DMA(())],
        compiler_params=pltpu.CompilerParams(dimension_semantics=("arbitrary",),
                                             vmem_limit_bytes=VMEM_LIMIT),
        name="peer_v",
    )(idx_t, w, tab)


def _final_body(x1_ref, o_ref, g_ref, y_ref):
    y_ref[...] = _rms(x1_ref[...] + o_ref[...], g_ref[...])


def _final(x1, o, g):
    t = x1.shape[0]
    tm = 512
    row = pl.BlockSpec((tm, D_MODEL), lambda i: (i, 0))
    return pl.pallas_call(
        _final_body,
        grid=(t // tm,),
        in_specs=[row, row, pl.BlockSpec((1, D_MODEL), lambda i: (0, 0))],
        out_specs=row,
        out_shape=jax.ShapeDtypeStruct((t, D_MODEL), F32),
        compiler_params=pltpu.CompilerParams(dimension_semantics=("parallel",)),
        name="final_norm",
    )(x1, o, g)


def _rel_bucket(rel):
    nb = NUM_BUCKETS // 2
    ret = jnp.where(rel > 0, nb, 0).astype(jnp.int32)
    n = jnp.abs(rel)
    max_exact = nb // 2
    nf = jnp.maximum(n, 1).astype(F32)
    large = max_exact + (jnp.log(nf / max_exact) / math.log(MAX_DISTANCE / max_exact) * (nb - max_exact)).astype(jnp.int32)
    large = jnp.minimum(large, nb - 1)
    return ret + jnp.where(n < max_exact, n, large)


def _bias_tables(rel_bias, tq):
    assert tq >= MAX_DISTANCE
    qq = jnp.arange(tq, dtype=jnp.int32)[:, None]
    kk = jnp.arange(tq, dtype=jnp.int32)[None, :]
    rel = jnp.stack([(d * tq + kk) - qq for d in (-1, 0, 1)])
    tiles = jnp.transpose(rel_bias[_rel_bucket(rel)], (3, 0, 1, 2)).astype(F32)
    far = rel_bias[_rel_bucket(jnp.array([-MAX_DISTANCE, MAX_DISTANCE], jnp.int32))].astype(F32)
    return tiles, far


def kernel(x_prompt, x_sample, norm1_g, w_in, b_gate, conv_w, conv_b, lru_wa_f, lru_ba_f, lru_wx_f, lru_bx_f, lru_lam_f, lru_wa_b, lru_ba_b, lru_wx_b, lru_bx_b, lru_lam_b, lam_q1, lam_k1, lam_q2, lam_k2, subln_g, rel_bias, w_rnn_out, w_attn_out, w_out, norm2_g, peer_wq, peer_keys, peer_u, peer_v, final_g):
    groups = [x_prompt.shape[:2], x_sample.shape[:2]]
    x2d = jnp.concatenate([x_prompt.reshape(-1, D_MODEL), x_sample.reshape(-1, D_MODEL)], axis=0)
    t = x2d.shape[0]
    l = 0
    row = lambda a: a.reshape(1, -1)

    proj = _inproj(x2d, row(norm1_g[l]), w_in[l].astype(BF16))

    btiles, far = _bias_tables(rel_bias, 512)
    lamv = jnp.stack([lam_q1[l], lam_k1[l], lam_q2[l], lam_k2[l]])
    lru_f = (conv_w[l], row(conv_b[l]), lru_wa_f[l].astype(BF16), row(lru_ba_f[l]),
             lru_wx_f[l].astype(BF16), row(lru_bx_f[l]), row(lru_lam_f[l]))
    lru_b = (conv_w[l], row(conv_b[l]), lru_wa_b[l].astype(BF16), row(lru_ba_b[l]),
             lru_wx_b[l].astype(BF16), row(lru_bx_b[l]), row(lru_lam_b[l]))
    hf, hb, attn = [], [], []
    row0 = 0
    for nb, s in groups:
        hf.append(_lru(proj, row0, nb, s, *lru_f, reverse=False))
        hb.append(_lru(proj, row0, nb, s, *lru_b, reverse=True))
        attn.append(_attention(proj, row0, nb, s, far, btiles, lamv, row(subln_g[l])))
        row0 += nb * s
    hf, hb, attn = (jnp.concatenate(a, axis=0) for a in (hf, hb, attn))

    x1, xn2, qp = _merge(x2d, hf, hb, proj, attn, b_gate[l].reshape(2, D_MODEL),
                         w_rnn_out[l].astype(BF16), w_attn_out[l].astype(BF16), w_out[l].astype(BF16),
                         row(norm2_g[l]), peer_wq[l].astype(BF16))

    idx, g = _router(qp, peer_keys[l].astype(BF16))
    idx_t = idx.T
    g_t = g.T
    tab_u = _pack_table(peer_u[l].reshape(N_EXPERTS, SUBLANES, LANES))
    tab_v = _pack_table(peer_v[l].reshape(N_EXPERTS, SUBLANES, LANES))
    w = _peer_u(idx_t, xn2.reshape(t, SUBLANES, LANES), g_t, tab_u)
    o = _peer_v(idx_t, w, tab_v).reshape(t, D_MODEL)

    y = _final(x1, o, row(final_g))
    n0 = groups[0][0] * groups[0][1]
    return (y[:n0].reshape(x_prompt.shape), y[n0:].reshape(x_sample.shape))
```

```python
import functools
import math

import jax
import jax.numpy as jnp
from jax import lax
from jax.experimental import pallas as pl
from jax.experimental.pallas import tpu as pltpu

F32 = jnp.float32
BF16 = jnp.bfloat16

D_MODEL = 1024
D_RNN = 1024
LRU_BLOCKS = 4
LRU_BW = D_RNN // LRU_BLOCKS
LRU_C = 8.0
CONV_W = 4
N_HEADS = 8
HEAD_DIM = 64
V_DIM = 2 * HEAD_DIM
ATTN_W = N_HEADS * V_DIM
NUM_BUCKETS = 32
MAX_DISTANCE = 128
PEER_HEADS = 8
N_KEYS = 128
N_EXPERTS = N_KEYS * N_KEYS
PEER_TOPK = 16
D_KEY = 256
D_HALF = D_KEY // 2
N_SEL = PEER_HEADS * PEER_TOPK
IN_W = 2 * D_RNN + 3 * ATTN_W + 2 * D_MODEL
EPS = 1e-6
LAM_INIT = 0.8 - 0.6 * math.exp(-0.3 * 0)
LOG2E = math.log2(math.e)
NT_DIMS = (((1,), (1,)), ((), ()))

COL_XRNN, COL_GRNN, COL_Q, COL_K, COL_V, COL_GLR, COL_GLA = range(7)

SUBLANES = 8
LANES = 128
ROW_WORDS = D_MODEL // 2
ROW_SUB = ROW_WORDS // LANES
VMEM_LIMIT = 56 * 1024 * 1024


def _rms(x, g):
    return x * lax.rsqrt(jnp.mean(x * x, axis=-1, keepdims=True) + EPS) * g


def _inproj_body(x_ref, g_ref, w_ref, o_ref, xn_ref):
    @pl.when(pl.program_id(1) == 0)
    def _():
        xn_ref[...] = _rms(x_ref[...], g_ref[...]).astype(BF16)

    o_ref[...] = jnp.dot(xn_ref[...], w_ref[...], preferred_element_type=F32)


def _inproj(x2d, g, w):
    t = x2d.shape[0]
    tm, tn = 512, 1024
    return pl.pallas_call(
        _inproj_body,
        grid=(t // tm, IN_W // tn),
        in_specs=[
            pl.BlockSpec((tm, D_MODEL), lambda i, j: (i, 0)),
            pl.BlockSpec((1, D_MODEL), lambda i, j: (0, 0)),
            pl.BlockSpec((D_MODEL, tn), lambda i, j: (0, j)),
        ],
        out_specs=pl.BlockSpec((tm, tn), lambda i, j: (i, j)),
        out_shape=jax.ShapeDtypeStruct((t, IN_W), F32),
        scratch_shapes=[pltpu.VMEM((tm, D_MODEL), BF16)],
        compiler_params=pltpu.CompilerParams(dimension_semantics=("parallel", "arbitrary")),
        name="inproj",
    )(x2d, g, w)


def _lru_body(xc_ref, xp_ref, xn_ref, cw_ref, cb_ref, wa_ref, ba_ref, wx_ref, bx_ref, lam_ref,
              h_ref, a_s, b_s, carry, *, reverse, nchunks, tc):
    i = pl.program_id(1)
    c = (nchunks - 1 - i) if reverse else i
    x = xc_ref[...]
    rows = lax.broadcasted_iota(jnp.int32, (tc, D_RNN), 0)
    prev = jnp.where(c > 0, xp_ref[SUBLANES - 1:SUBLANES, :], 0.0)
    nxt0 = jnp.where(c < nchunks - 1, xn_ref[0:1, :], 0.0)
    nxt1 = jnp.where(c < nchunks - 1, xn_ref[1:2, :], 0.0)
    xm1 = jnp.where(rows == 0, prev, pltpu.roll(x, 1, 0))
    xp1 = jnp.where(rows == tc - 1, nxt0, pltpu.roll(x, tc - 1, 0))
    xp2 = jnp.where(rows == tc - 2, nxt0, jnp.where(rows == tc - 1, nxt1, pltpu.roll(x, tc - 2, 0)))
    cw = cw_ref[...]
    xc = cb_ref[...] + xm1 * cw[0:1] + x * cw[1:2] + xp1 * cw[2:3] + xp2 * cw[3:4]

    xcb = xc.astype(BF16)
    r_parts, i_parts = [], []
    for n in range(LRU_BLOCKS):
        xs = xcb[:, n * LRU_BW:(n + 1) * LRU_BW]
        r_parts.append(jnp.dot(xs, wa_ref[n], preferred_element_type=F32))
        i_parts.append(jnp.dot(xs, wx_ref[n], preferred_element_type=F32))
    r = jax.nn.sigmoid(jnp.concatenate(r_parts, axis=1) + ba_ref[...])
    ig = jax.nn.sigmoid(jnp.concatenate(i_parts, axis=1) + bx_ref[...])
    z = -lam_ref[...]
    softplus = jnp.maximum(z, 0.0) + jnp.log1p(jnp.exp(-jnp.abs(z)))
    log_a = -LRU_C * r * softplus
    a = jnp.exp(log_a)
    mult = jnp.sqrt(-jnp.tanh(log_a) * (a * a + 1.0))
    edge_row = tc - 1 if reverse else 0
    edge_chunk = nchunks - 1 if reverse else 0
    mult = jnp.where(rows == jnp.where(c == edge_chunk, edge_row, -1), 1.0, mult)
    a_s[...] = a
    b_s[...] = mult * ig * xc

    row8 = lax.broadcasted_iota(jnp.int32, (SUBLANES, D_RNN), 0)
    ngroups = tc // SUBLANES
    h0 = jnp.where(i == 0, 0.0, carry[...])

    def step(gi, h):
        g = (ngroups - 1 - gi) if reverse else gi
        off = pl.multiple_of(g * SUBLANES, SUBLANES)
        av = a_s[pl.ds(off, SUBLANES), :]
        bv = b_s[pl.ds(off, SUBLANES), :]
        for s in (1, 2, 4):
            if reverse:
                keep = row8 < SUBLANES - s
                shift = SUBLANES - s
            else:
                keep = row8 >= s
                shift = s
            a_sh = jnp.where(keep, pltpu.roll(av, shift, 0), 1.0)
            b_sh = jnp.where(keep, pltpu.roll(bv, shift, 0), 0.0)
            bv = av * b_sh + bv
            av = av * a_sh
        hv = av * h + bv
        h_ref[pl.ds(off, SUBLANES), :] = hv
        last = hv[0:1, :] if reverse else hv[SUBLANES - 1:SUBLANES, :]
        return jnp.broadcast_to(last, (SUBLANES, D_RNN))

    carry[...] = lax.fori_loop(0, ngroups, step, h0)


def _lru(proj, row0, nb, s, cw, cb, wa, ba, wx, bx, lam, reverse):
    tc = 256
    nchunks = s // tc
    t_all = proj.shape[0]
    blk0 = row0 // tc
    last8 = t_all // SUBLANES - 1

    def cidx(i):
        return (nchunks - 1 - i) if reverse else i

    def cur(b, i):
        return (blk0 + b * nchunks + cidx(i), COL_XRNN)

    def prev8(b, i):
        r = (blk0 + b * nchunks + cidx(i)) * (tc // SUBLANES) - 1
        return (jnp.maximum(r, 0), COL_XRNN)

    def next8(b, i):
        r = (blk0 + b * nchunks + cidx(i) + 1) * (tc // SUBLANES)
        return (jnp.minimum(r, last8), COL_XRNN)

    full = lambda shape: pl.BlockSpec(shape, lambda b, i: (0,) * len(shape))
    body = functools.partial(_lru_body, reverse=reverse, nchunks=nchunks, tc=tc)
    return pl.pallas_call(
        body,
        grid=(nb, nchunks),
        in_specs=[
            pl.BlockSpec((tc, D_RNN), cur),
            pl.BlockSpec((SUBLANES, D_RNN), prev8),
            pl.BlockSpec((SUBLANES, D_RNN), next8),
            full((CONV_W, D_RNN)), full((1, D_RNN)),
            full((LRU_BLOCKS, LRU_BW, LRU_BW)), full((1, D_RNN)),
            full((LRU_BLOCKS, LRU_BW, LRU_BW)), full((1, D_RNN)),
            full((1, D_RNN)),
        ],
        out_specs=pl.BlockSpec((tc, D_RNN), lambda b, i: (b * nchunks + cidx(i), 0)),
        out_shape=jax.ShapeDtypeStruct((nb * s, D_RNN), F32),
        scratch_shapes=[pltpu.VMEM((tc, D_RNN), F32), pltpu.VMEM((tc, D_RNN), F32),
                        pltpu.VMEM((SUBLANES, D_RNN), F32)],
        compiler_params=pltpu.CompilerParams(dimension_semantics=("arbitrary", "arbitrary")),
        name="lru_bwd" if reverse else "lru_fwd",
    )(proj, proj, proj, cw, cb, wa, ba, wx, bx, lam)


def _attn_body(far_ref, q_ref, k_ref, v_ref, bt_ref, lamv_ref, g_ref, o_ref,
               q1_s, q2_s, m1_s, l1_s, acc1_s, m2_s, l2_s, acc2_s, *, nkv):
    h = pl.program_id(1)
    i = pl.program_id(2)
    j = pl.program_id(3)

    @pl.when(j == 0)
    def _():
        q = q_ref[...] * (HEAD_DIM ** -0.5 * LOG2E)
        lane = lax.broadcasted_iota(jnp.int32, q.shape, 1)
        q1_s[...] = jnp.where(lane < HEAD_DIM, q, 0.0).astype(BF16)
        q2_s[...] = jnp.where(lane >= HEAD_DIM, q, 0.0).astype(BF16)
        for m_s, l_s, acc_s in ((m1_s, l1_s, acc1_s), (m2_s, l2_s, acc2_s)):
            m_s[...] = jnp.full(m_s.shape, -jnp.inf, F32)
            l_s[...] = jnp.zeros(l_s.shape, F32)
            acc_s[...] = jnp.zeros(acc_s.shape, F32)

    k = k_ref[...].astype(BF16)
    vt = v_ref[...].T.astype(BF16)

    def update(bias_tile, bias_const):
        for q_s, m_s, l_s, acc_s in ((q1_s, m1_s, l1_s, acc1_s), (q2_s, m2_s, l2_s, acc2_s)):
            s = lax.dot_general(k, q_s[...], NT_DIMS, preferred_element_type=F32)
            if bias_tile is not None:
                s = s + bias_tile
            m_old = m_s[...]
            m_new = jnp.maximum(m_old, jnp.max(s, axis=0, keepdims=True) + bias_const)
            alpha = jnp.exp2(m_old - m_new)
            p = jnp.exp2(s - (m_new - bias_const))
            l_s[...] = alpha * l_s[...] + jnp.sum(p, axis=0, keepdims=True)
            acc_s[...] = alpha * acc_s[...] + jnp.dot(vt, p.astype(BF16), preferred_element_type=F32)
            m_s[...] = m_new

    near = jnp.abs(j - i) <= 1

    @pl.when(near)
    def _():
        update(bt_ref[0, 0], 0.0)

    @pl.when(jnp.logical_not(near))
    def _():
        update(None, jnp.where(j < i, far_ref[0, h], far_ref[1, h]))

    @pl.when(j == nkv - 1)
    def _():
        lv = lamv_ref[...]
        lam = (jnp.exp(jnp.sum(lv[0:1] * lv[1:2], axis=-1, keepdims=True))
               - jnp.exp(jnp.sum(lv[2:3] * lv[3:4], axis=-1, keepdims=True)) + LAM_INIT)
        o = acc1_s[...] / l1_s[...] - lam * (acc2_s[...] / l2_s[...])
        o_ref[...] = _rms(o.T, g_ref[...]) * (1.0 - LAM_INIT)


def _attention(proj, row0, nb, s, far, btiles, lamv, subln_g):
    tq = btiles.shape[-1]
    nq = s // tq
    blk0 = row0 // tq
    hb = V_DIM // LANES

    def qmap(b, h, i, j):
        return (blk0 + b * nq + i, COL_Q * N_HEADS * hb + h)

    def kmap(b, h, i, j):
        return (blk0 + b * nq + j, COL_K * N_HEADS * hb + h)

    def vmap(b, h, i, j):
        return (blk0 + b * nq + j, COL_V * N_HEADS * hb + h)

    def bmap(b, h, i, j):
        return (h, jnp.clip(j - i, -1, 1) + 1, 0, 0)

    body = functools.partial(_attn_body, nkv=nq)
    stat = pltpu.VMEM((1, tq), F32)
    acc = pltpu.VMEM((V_DIM, tq), F32)
    return pl.pallas_call(
        body,
        grid=(nb, N_HEADS, nq, nq),
        in_specs=[
            pl.BlockSpec(memory_space=pltpu.SMEM),
            pl.BlockSpec((tq, V_DIM), qmap),
            pl.BlockSpec((tq, V_DIM), kmap),
            pl.BlockSpec((tq, V_DIM), vmap),
            pl.BlockSpec((1, 1, tq, tq), bmap),
            pl.BlockSpec((4, HEAD_DIM), lambda b, h, i, j: (0, 0)),
            pl.BlockSpec((1, V_DIM), lambda b, h, i, j: (0, 0)),
        ],
        out_specs=pl.BlockSpec((tq, V_DIM), lambda b, h, i, j: (b * nq + i, h)),
        out_shape=jax.ShapeDtypeStruct((nb * s, ATTN_W), F32),
        scratch_shapes=[pltpu.VMEM((tq, V_DIM), BF16), pltpu.VMEM((tq, V_DIM), BF16),
                        stat, stat, acc, stat, stat, acc],
        compiler_params=pltpu.CompilerParams(
            dimension_semantics=("parallel", "parallel", "parallel", "arbitrary"),
            vmem_limit_bytes=VMEM_LIMIT),
        name="diff_attn",
    )(far, proj, proj, proj, btiles, lamv, subln_g)


def _merge_body(x_ref, hf_ref, hb_ref, grnn_ref, attn_ref, glr_ref, gla_ref, bg_ref,
                wr_ref, wa_ref, wo_ref, n2_ref, wq_ref, x1_ref, xn2_ref, qp_ref):
    hg = ((hf_ref[...] + hb_ref[...]) * jax.nn.gelu(grnn_ref[...])).astype(BF16)
    y_rnn = jnp.dot(hg, wr_ref[...], preferred_element_type=F32)
    y_attn = jnp.dot(attn_ref[...].astype(BF16), wa_ref[...], preferred_element_type=F32)
    bg = bg_ref[...]
    g_r = jax.nn.sigmoid(glr_ref[...] + bg[0:1])
    g_a = jax.nn.sigmoid(gla_ref[...] + bg[1:2])
    merged = (g_r * y_rnn + g_a * y_attn).astype(BF16)
    x1 = x_ref[...] + jnp.dot(merged, wo_ref[...], preferred_element_type=F32)
    x1_ref[...] = x1
    xn2 = _rms(x1, n2_ref[...])
    xn2_ref[...] = xn2
    qp_ref[...] = jnp.dot(xn2.astype(BF16), wq_ref[...], preferred_element_type=F32)


def _merge(x2d, hf, hb, proj, attn, bg, wr, wa, wo, n2, wq):
    t = x2d.shape[0]
    tm = 256
    row = lambda c: pl.BlockSpec((tm, D_MODEL), lambda i: (i, c))
    full = lambda shape: pl.BlockSpec(shape, lambda i: (0,) * len(shape))
    qw = PEER_HEADS * D_KEY
    return pl.pallas_call(
        _merge_body,
        grid=(t // tm,),
        in_specs=[row(0), row(0), row(0), row(COL_GRNN), row(0), row(COL_GLR), row(COL_GLA),
                  full((2, D_MODEL)), full((D_RNN, D_MODEL)), full((ATTN_W, D_MODEL)),
                  full((D_MODEL, D_MODEL)), full((1, D_MODEL)), full((D_MODEL, qw))],
        out_specs=[row(0), row(0), pl.BlockSpec((tm, qw), lambda i: (i, 0))],
        out_shape=[jax.ShapeDtypeStruct((t, D_MODEL), F32), jax.ShapeDtypeStruct((t, D_MODEL), F32),
                   jax.ShapeDtypeStruct((t, qw), F32)],
        compiler_params=pltpu.CompilerParams(dimension_semantics=("parallel",),
                                             vmem_limit_bytes=VMEM_LIMIT),
        name="merge_proj",
    )(x2d, hf, hb, proj, attn, proj, proj, bg, wr, wa, wo, n2, wq)


def _topk_rows(s, k):
    n = s.shape[0]
    iota = lax.broadcasted_iota(jnp.int32, s.shape, 0).astype(F32)
    slot = lax.broadcasted_iota(jnp.int32, (k, s.shape[1]), 0)
    vals = jnp.zeros((k, s.shape[1]), F32)
    idxs = jnp.zeros((k, s.shape[1]), F32)
    for t in range(k):
        m = jnp.max(s, axis=0, keepdims=True)
        pos = jnp.min(jnp.where(s == m, iota, float(n)), axis=0, keepdims=True)
        vals = jnp.where(slot == t, m, vals)
        idxs = jnp.where(slot == t, pos, idxs)
        s = jnp.where(iota == pos, -jnp.inf, s)
    return vals, idxs


def _router_body(qp_ref, keys_ref, idx_ref, g_ref):
    slot = lax.broadcasted_iota(jnp.int32, (PEER_TOPK, qp_ref.shape[0]), 0)
    for h in range(PEER_HEADS):
        sv, si = [], []
        for p in range(2):
            c0 = (h * 2 + p) * D_HALF
            q = qp_ref[:, c0:c0 + D_HALF].astype(BF16)
            s = lax.dot_general(keys_ref[h, p], q, (((1,), (1,)), ((), ())), preferred_element_type=F32)
            v, ix = _topk_rows(s, PEER_TOPK)
            sv.append(v)
            si.append(ix)
        cand = jnp.concatenate([sv[0][a:a + 1, :] + sv[1] for a in range(PEER_TOPK)], axis=0)
        top_s, pos = _topk_rows(cand, PEER_TOPK)
        pos = pos.astype(jnp.int32)
        pa = pos >> 4
        pb = pos & (PEER_TOPK - 1)
        idx = jnp.zeros(slot.shape, F32)
        for t in range(PEER_TOPK):
            i1 = jnp.sum(jnp.where(slot == pa[t:t + 1, :], si[0], 0.0), axis=0, keepdims=True)
            i2 = jnp.sum(jnp.where(slot == pb[t:t + 1, :], si[1], 0.0), axis=0, keepdims=True)
            idx = jnp.where(slot == t, i1 * N_KEYS + i2, idx)
        idx = idx.astype(jnp.int32)
        e = jnp.exp(top_s - jnp.max(top_s, axis=0, keepdims=True))
        g = e / jnp.sum(e, axis=0, keepdims=True)
        idx_ref[h * PEER_TOPK:(h + 1) * PEER_TOPK, :] = idx
        g_ref[h * PEER_TOPK:(h + 1) * PEER_TOPK, :] = g


def _router(qp, keys):
    t = qp.shape[0]
    c = 256
    qw = PEER_HEADS * D_KEY
    return pl.pallas_call(
        _router_body,
        grid=(t // c,),
        in_specs=[pl.BlockSpec((c, qw), lambda i: (i, 0)),
                  pl.BlockSpec((PEER_HEADS, 2, N_KEYS, D_HALF), lambda i: (0, 0, 0, 0))],
        out_specs=[pl.BlockSpec((N_SEL, c), lambda i: (0, i)), pl.BlockSpec((N_SEL, c), lambda i: (0, i))],
        out_shape=[jax.ShapeDtypeStruct((N_SEL, t), jnp.int32), jax.ShapeDtypeStruct((N_SEL, t), F32)],
        compiler_params=pltpu.CompilerParams(dimension_semantics=("parallel",)),
        name="peer_router",
    )(qp, keys)


def _pack_body(t_ref, o_ref):
    o_ref[...] = pltpu.bitcast(t_ref[...].astype(BF16), jnp.int32)


def _pack_table(tab3):
    n = tab3.shape[0]
    r = 512
    return pl.pallas_call(
        _pack_body,
        grid=(n // r,),
        in_specs=[pl.BlockSpec((r, SUBLANES, LANES), lambda i: (i, 0, 0))],
        out_specs=pl.BlockSpec((r, ROW_SUB, LANES), lambda i: (i, 0, 0)),
        out_shape=jax.ShapeDtypeStruct((n, ROW_SUB, LANES), jnp.int32),
        compiler_params=pltpu.CompilerParams(dimension_semantics=("parallel",)),
        name="peer_pack",
    )(tab3)


PEER_UNROLL = SUBLANES


def _load_table(tab_hbm, tab_s, sem):
    @pl.when(pl.program_id(0) == 0)
    def _():
        cp = pltpu.make_async_copy(tab_hbm, tab_s, sem)
        cp.start()
        cp.wait()


def _gather_rows(tab_s, idx_ref, t, m_ref):
    for j in range(N_SEL):
        m_ref[j * ROW_SUB:(j + 1) * ROW_SUB, :] = tab_s[idx_ref[t, j]]


def _split3_bf16(x):
    hi = x.astype(BF16).astype(F32)
    r1 = x - hi
    mid = r1.astype(BF16).astype(F32)
    lo = r1 - mid
    return jnp.concatenate([hi, mid, lo, jnp.zeros_like(x)], axis=0).astype(BF16)


def _sum3(y):
    return y[0:SUBLANES] + y[SUBLANES:2 * SUBLANES] + y[2 * SUBLANES:3 * SUBLANES]


def _diag_mask():
    p = lax.broadcasted_iota(jnp.int32, (SUBLANES, N_SEL * SUBLANES), 0)
    c = lax.broadcasted_iota(jnp.int32, (SUBLANES, N_SEL * SUBLANES), 1)
    return (c & (SUBLANES - 1)) == p


def _peer_u_body(idx_ref, x_ref, g_ref, tab_hbm, w_ref, tab_s, m_s, fold_s, s_s, sem, *, tb):
    _load_table(tab_hbm, tab_s, sem)

    @pl.when(pl.program_id(0) == 0)
    def _():
        r = lax.broadcasted_iota(jnp.int32, fold_s.shape, 0)
        c = lax.broadcasted_iota(jnp.int32, fold_s.shape, 1)
        fold_s[...] = jnp.where((r >> 3) == c, 1.0, 0.0).astype(BF16)

    diag = _diag_mask()

    def group(gi, carry):
        base = gi * PEER_UNROLL
        for u in range(PEER_UNROLL):
            _gather_rows(tab_s, idx_ref, base + u, m_s.at[u])
        for u in range(PEER_UNROLL):
            rows = pltpu.bitcast(m_s[u], BF16)
            p = _sum3(lax.dot_general(_split3_bf16(x_ref[base + u]), rows, NT_DIMS, preferred_element_type=F32))
            p = jnp.where(diag, p, 0.0)
            s_s[base + u] = _sum3(jnp.dot(_split3_bf16(p), fold_s[...], preferred_element_type=F32))
        return carry

    lax.fori_loop(0, tb // PEER_UNROLL, group, 0)
    s = s_s[:, 0, :]
    for q in range(1, SUBLANES):
        s = s + s_s[:, q, :]
    w_ref[...] = g_ref[...] * jax.nn.gelu(s)


def _peer_u(idx_t, x3, g_t, tab):
    t = idx_t.shape[0]
    tb = 128
    body = functools.partial(_peer_u_body, tb=tb)
    return pl.pallas_call(
        body,
        grid=(t // tb,),
        in_specs=[pl.BlockSpec((tb, N_SEL), lambda i: (i, 0), memory_space=pltpu.SMEM),
                  pl.BlockSpec((tb, SUBLANES, LANES), lambda i: (i, 0, 0)),
                  pl.BlockSpec((tb, N_SEL), lambda i: (i, 0)),
                  pl.BlockSpec(memory_space=pl.ANY)],
        out_specs=pl.BlockSpec((tb, N_SEL), lambda i: (i, 0)),
        out_shape=jax.ShapeDtypeStruct((t, N_SEL), F32),
        scratch_shapes=[pltpu.VMEM((N_EXPERTS, ROW_SUB, LANES), jnp.int32),
                        pltpu.VMEM((PEER_UNROLL, N_SEL * ROW_SUB, LANES), jnp.int32),
                        pltpu.VMEM((N_SEL * SUBLANES, N_SEL), BF16),
                        pltpu.VMEM((tb, SUBLANES, LANES), F32),
                        pltpu.SemaphoreType.DMA(())],
        compiler_params=pltpu.CompilerParams(dimension_semantics=("arbitrary",),
                                             vmem_limit_bytes=VMEM_LIMIT),
        name="peer_u",
    )(idx_t, x3, g_t, tab)


def _peer_v_body(idx_ref, w_ref, tab_hbm, o_ref, tab_s, m_s, spread_s, sem, *, tb):
    _load_table(tab_hbm, tab_s, sem)

    @pl.when(pl.program_id(0) == 0)
    def _():
        r = lax.broadcasted_iota(jnp.int32, spread_s.shape, 0)
        c = lax.broadcasted_iota(jnp.int32, spread_s.shape, 1)
        spread_s[...] = jnp.where((c >> 3) == r, 1.0, 0.0).astype(BF16)

    diag = _diag_mask()
    zero = jnp.zeros((SUBLANES, N_SEL * SUBLANES), F32)

    def group(gi, carry):
        base = pl.multiple_of(gi * PEER_UNROLL, PEER_UNROLL)
        for u in range(PEER_UNROLL):
            _gather_rows(tab_s, idx_ref, base + u, m_s.at[u])
        e = jnp.dot(_split3_bf16(w_ref[pl.ds(base, PEER_UNROLL), :]), spread_s[...], preferred_element_type=F32)
        for u in range(PEER_UNROLL):
            rows = pltpu.bitcast(m_s[u], BF16)
            terms = [jnp.where(diag, jnp.broadcast_to(e[SUBLANES * k + u:SUBLANES * k + u + 1, :], diag.shape), 0.0)
                     for k in range(3)]
            lhs = jnp.concatenate(terms + [zero], axis=0).astype(BF16)
            o_ref[base + u] = _sum3(jnp.dot(lhs, rows, preferred_element_type=F32))
        return carry

    lax.fori_loop(0, tb // PEER_UNROLL, group, 0)


def _peer_v(idx_t, w, tab):
    t = idx_t.shape[0]
    tb = 128
    body = functools.partial(_peer_v_body, tb=tb)
    return pl.pallas_call(
        body,
        grid=(t // tb,),
        in_specs=[pl.BlockSpec((tb, N_SEL), lambda i: (i, 0), memory_space=pltpu.SMEM),
                  pl.BlockSpec((tb, N_SEL), lambda i: (i, 0)),
                  pl.BlockSpec(memory_space=pl.ANY)],
        out_specs=pl.BlockSpec((tb, SUBLANES, LANES), lambda i: (i, 0, 0)),
        out_shape=jax.ShapeDtypeStruct((t, SUBLANES, LANES), F32),
        scratch_shapes=[pltpu.VMEM((N_EXPERTS, ROW_SUB, LANES), jnp.int32),
                        pltpu.VMEM((PEER_UNROLL, N_SEL * ROW_SUB, LANES), jnp.int32),
                        pltpu.VMEM((N_SEL, N_SEL * SUBLANES), BF16),
                        pltpu.SemaphoreType.DMA(())],
        compiler_params=pltpu.CompilerParams(dimension_semantics=("arbitrary",),
                                             vmem_limit_bytes=VMEM_LIMIT),
        name="peer_v",
    )(idx_t, w, tab)


def _final_body(x1_ref, o_ref, g_ref, y_ref):
    y_ref[...] = _rms(x1_ref[...] + o_ref[...], g_ref[...])


def _final(x1, o, g):
    t = x1.shape[0]
    tm = 512
    row = pl.BlockSpec((tm, D_MODEL), lambda i: (i, 0))
    return pl.pallas_call(
        _final_body,
        grid=(t // tm,),
        in_specs=[row, row, pl.BlockSpec((1, D_MODEL), lambda i: (0, 0))],
        out_specs=row,
        out_shape=jax.ShapeDtypeStruct((t, D_MODEL), F32),
        compiler_params=pltpu.CompilerParams(dimension_semantics=("parallel",)),
        name="final_norm",
    )(x1, o, g)


def _rel_bucket(rel):
    nb = NUM_BUCKETS // 2
    ret = jnp.where(rel > 0, nb, 0).astype(jnp.int32)
    n = jnp.abs(rel)
    max_exact = nb // 2
    nf = jnp.maximum(n, 1).astype(F32)
    large = max_exact + (jnp.log(nf / max_exact) / math.log(MAX_DISTANCE / max_exact) * (nb - max_exact)).astype(jnp.int32)
    large = jnp.minimum(large, nb - 1)
    return ret + jnp.where(n < max_exact, n, large)


def _bias_tables(rel_bias, tq):
    assert tq >= MAX_DISTANCE
    m = jnp.arange(2 * tq - 1, dtype=jnp.int32)
    rel = jnp.stack([d * tq + m - (tq - 1) for d in (-1, 0, 1)])
    w = jnp.transpose(rel_bias[_rel_bucket(rel)], (2, 0, 1)).astype(F32) * LOG2E
    period = 2 * tq
    wp = jnp.pad(w, ((0, 0), (0, 0), (0, 1)))
    rep = jnp.tile(wp, (1, 1, tq + 1))[..., :tq * (period + 1)]
    hankel = rep.reshape(w.shape[:2] + (tq, period + 1))[..., :tq]
    tiles = hankel[..., ::-1]
    far = rel_bias[_rel_bucket(jnp.array([-MAX_DISTANCE, MAX_DISTANCE], jnp.int32))].astype(F32) * LOG2E
    return tiles, far


def kernel(x_prompt, x_sample, norm1_g, w_in, b_gate, conv_w, conv_b, lru_wa_f, lru_ba_f, lru_wx_f, lru_bx_f, lru_lam_f, lru_wa_b, lru_ba_b, lru_wx_b, lru_bx_b, lru_lam_b, lam_q1, lam_k1, lam_q2, lam_k2, subln_g, rel_bias, w_rnn_out, w_attn_out, w_out, norm2_g, peer_wq, peer_keys, peer_u, peer_v, final_g):
    groups = [x_prompt.shape[:2], x_sample.shape[:2]]
    x2d = jnp.concatenate([x_prompt.reshape(-1, D_MODEL), x_sample.reshape(-1, D_MODEL)], axis=0)
    t = x2d.shape[0]
    l = 0
    row = lambda a: a.reshape(1, -1)

    proj = _inproj(x2d, row(norm1_g[l]), w_in[l].astype(BF16))

    btiles, far = _bias_tables(rel_bias, 512)
    lamv = jnp.stack([lam_q1[l], lam_k1[l], lam_q2[l], lam_k2[l]])
    lru_f = (conv_w[l], row(conv_b[l]), lru_wa_f[l].astype(BF16), row(lru_ba_f[l]),
             lru_wx_f[l].astype(BF16), row(lru_bx_f[l]), row(lru_lam_f[l]))
    lru_b = (conv_w[l], row(conv_b[l]), lru_wa_b[l].astype(BF16), row(lru_ba_b[l]),
             lru_wx_b[l].astype(BF16), row(lru_bx_b[l]), row(lru_lam_b[l]))
    hf, hb, attn = [], [], []
    row0 = 0
    for nb, s in groups:
        hf.append(_lru(proj, row0, nb, s, *lru_f, reverse=False))
        hb.append(_lru(proj, row0, nb, s, *lru_b, reverse=True))
        attn.append(_attention(proj, row0, nb, s, far, btiles, lamv, row(subln_g[l])))
        row0 += nb * s
    hf, hb, attn = (jnp.concatenate(a, axis=0) for a in (hf, hb, attn))

    x1, xn2, qp = _merge(x2d, hf, hb, proj, attn, b_gate[l].reshape(2, D_MODEL),
                         w_rnn_out[l].astype(BF16), w_attn_out[l].astype(BF16), w_out[l].astype(BF16),
                         row(norm2_g[l]), peer_wq[l].astype(BF16))

    idx, g = _router(qp, peer_keys[l].astype(BF16))
    idx_t = idx.T
    g_t = g.T
    tab_u = _pack_table(peer_u[l].reshape(N_EXPERTS, SUBLANES, LANES))
    tab_v = _pack_table(peer_v[l].reshape(N_EXPERTS, SUBLANES, LANES))
    w = _peer_u(idx_t, xn2.reshape(t, SUBLANES, LANES), g_t, tab_u)
    o = _peer_v(idx_t, w, tab_v).reshape(t, D_MODEL)

    y = _final(x1, o, row(final_g))
    n0 = groups[0][0] * groups[0][1]
    return (y[:n0].reshape(x_prompt.shape), y[n0:].reshape(x_sample.shape))
```

```python
import functools
import math

import jax
import jax.numpy as jnp
from jax import lax
from jax.experimental import pallas as pl
from jax.experimental.pallas import tpu as pltpu

F32 = jnp.float32
BF16 = jnp.bfloat16

D_MODEL = 1024
D_RNN = 1024
LRU_BLOCKS = 4
LRU_BW = D_RNN // LRU_BLOCKS
LRU_C = 8.0
CONV_W = 4
N_HEADS = 8
HEAD_DIM = 64
V_DIM = 2 * HEAD_DIM
ATTN_W = N_HEADS * V_DIM
NUM_BUCKETS = 32
MAX_DISTANCE = 128
PEER_HEADS = 8
N_KEYS = 128
N_EXPERTS = N_KEYS * N_KEYS
PEER_TOPK = 16
D_KEY = 256
D_HALF = D_KEY // 2
N_SEL = PEER_HEADS * PEER_TOPK
IN_W = 2 * D_RNN + 3 * ATTN_W + 2 * D_MODEL
EPS = 1e-6
LAM_INIT = 0.8 - 0.6 * math.exp(-0.3 * 0)
LOG2E = math.log2(math.e)
NT_DIMS = (((1,), (1,)), ((), ()))

COL_XRNN, COL_GRNN, COL_GLR, COL_GLA = range(4)
QKV_COL0 = 2 * D_RNN
QKV_COL1 = QKV_COL0 + 3 * ATTN_W

SUBLANES = 8
LANES = 128
ROW_WORDS = D_MODEL // 2
ROW_SUB = ROW_WORDS // LANES
VMEM_LIMIT = 56 * 1024 * 1024


def _rms(x, g):
    return x * lax.rsqrt(jnp.mean(x * x, axis=-1, keepdims=True) + EPS) * g


def _inproj_body(x_ref, g_ref, w_ref, o_ref, xn_ref):
    @pl.when(pl.program_id(1) == 0)
    def _():
        xn_ref[...] = _rms(x_ref[...], g_ref[...]).astype(BF16)

    o_ref[...] = jnp.dot(xn_ref[...], w_ref[...], preferred_element_type=F32)


def _inproj(x2d, g, w):
    t = x2d.shape[0]
    n_out = w.shape[1]
    tm, tn = 512, 1024
    return pl.pallas_call(
        _inproj_body,
        grid=(t // tm, n_out // tn),
        in_specs=[
            pl.BlockSpec((tm, D_MODEL), lambda i, j: (i, 0)),
            pl.BlockSpec((1, D_MODEL), lambda i, j: (0, 0)),
            pl.BlockSpec((D_MODEL, tn), lambda i, j: (0, j)),
        ],
        out_specs=pl.BlockSpec((tm, tn), lambda i, j: (i, j)),
        out_shape=jax.ShapeDtypeStruct((t, n_out), F32),
        scratch_shapes=[pltpu.VMEM((tm, D_MODEL), BF16)],
        compiler_params=pltpu.CompilerParams(dimension_semantics=("parallel", "arbitrary")),
        name="inproj",
    )(x2d, g, w)


def _qkv_body(x_ref, g_ref, w_ref, qk_ref, vt_ref, xn_ref):
    j = pl.program_id(1)

    @pl.when(j == 0)
    def _():
        xn_ref[...] = _rms(x_ref[...], g_ref[...]).astype(BF16)

    res = jnp.dot(xn_ref[...], w_ref[...], preferred_element_type=F32)

    @pl.when(j < 2)
    def _():
        for h in range(N_HEADS):
            qk_ref[h] = res[:, h * V_DIM:(h + 1) * V_DIM].astype(BF16)

    @pl.when(j == 2)
    def _():
        for h in range(N_HEADS):
            vt_ref[h] = res[:, h * V_DIM:(h + 1) * V_DIM].T.astype(BF16)


def _qkv(x2d, g, w):
    t = x2d.shape[0]
    tm = 512
    return pl.pallas_call(
        _qkv_body,
        grid=(t // tm, 3),
        in_specs=[
            pl.BlockSpec((tm, D_MODEL), lambda i, j: (i, 0)),
            pl.BlockSpec((1, D_MODEL), lambda i, j: (0, 0)),
            pl.BlockSpec((D_MODEL, ATTN_W), lambda i, j: (0, j)),
        ],
        out_specs=[pl.BlockSpec((None, N_HEADS, tm, V_DIM), lambda i, j: (jnp.minimum(j, 1), 0, i, 0)),
                   pl.BlockSpec((N_HEADS, V_DIM, tm), lambda i, j: (0, 0, i))],
        out_shape=[jax.ShapeDtypeStruct((2, N_HEADS, t, V_DIM), BF16),
                   jax.ShapeDtypeStruct((N_HEADS, V_DIM, t), BF16)],
        scratch_shapes=[pltpu.VMEM((tm, D_MODEL), BF16)],
        compiler_params=pltpu.CompilerParams(dimension_semantics=("parallel", "arbitrary")),
        name="qkv_proj",
    )(x2d, g, w)


def _lru_body(xc_ref, xp_ref, xn_ref, cw_ref, cb_ref, wa_ref, ba_ref, wx_ref, bx_ref, lam_ref,
              h_ref, a_s, b_s, carry, *, reverse, nchunks, tc):
    i = pl.program_id(1)
    c = (nchunks - 1 - i) if reverse else i
    x = xc_ref[...]
    rows = lax.broadcasted_iota(jnp.int32, (tc, D_RNN), 0)
    prev = jnp.where(c > 0, xp_ref[SUBLANES - 1:SUBLANES, :], 0.0)
    nxt0 = jnp.where(c < nchunks - 1, xn_ref[0:1, :], 0.0)
    nxt1 = jnp.where(c < nchunks - 1, xn_ref[1:2, :], 0.0)
    xm1 = jnp.where(rows == 0, prev, pltpu.roll(x, 1, 0))
    xp1 = jnp.where(rows == tc - 1, nxt0, pltpu.roll(x, tc - 1, 0))
    xp2 = jnp.where(rows == tc - 2, nxt0, jnp.where(rows == tc - 1, nxt1, pltpu.roll(x, tc - 2, 0)))
    cw = cw_ref[...]
    xc = cb_ref[...] + xm1 * cw[0:1] + x * cw[1:2] + xp1 * cw[2:3] + xp2 * cw[3:4]

    xcb = xc.astype(BF16)
    r_parts, i_parts = [], []
    for n in range(LRU_BLOCKS):
        xs = xcb[:, n * LRU_BW:(n + 1) * LRU_BW]
        r_parts.append(jnp.dot(xs, wa_ref[n], preferred_element_type=F32))
        i_parts.append(jnp.dot(xs, wx_ref[n], preferred_element_type=F32))
    r = jax.nn.sigmoid(jnp.concatenate(r_parts, axis=1) + ba_ref[...])
    ig = jax.nn.sigmoid(jnp.concatenate(i_parts, axis=1) + bx_ref[...])
    z = -lam_ref[...]
    softplus = jnp.maximum(z, 0.0) + jnp.log1p(jnp.exp(-jnp.abs(z)))
    log_a = -LRU_C * r * softplus
    a = jnp.exp(log_a)
    mult = jnp.sqrt(-jnp.tanh(log_a) * (a * a + 1.0))
    edge_row = tc - 1 if reverse else 0
    edge_chunk = nchunks - 1 if reverse else 0
    mult = jnp.where(rows == jnp.where(c == edge_chunk, edge_row, -1), 1.0, mult)
    a_s[...] = a
    b_s[...] = mult * ig * xc

    row8 = lax.broadcasted_iota(jnp.int32, (SUBLANES, D_RNN), 0)
    ngroups = tc // SUBLANES
    h0 = jnp.where(i == 0, 0.0, carry[...])

    def step(gi, h):
        g = (ngroups - 1 - gi) if reverse else gi
        off = pl.multiple_of(g * SUBLANES, SUBLANES)
        av = a_s[pl.ds(off, SUBLANES), :]
        bv = b_s[pl.ds(off, SUBLANES), :]
        for s in (1, 2, 4):
            if reverse:
                keep = row8 < SUBLANES - s
                shift = SUBLANES - s
            else:
                keep = row8 >= s
                shift = s
            a_sh = jnp.where(keep, pltpu.roll(av, shift, 0), 1.0)
            b_sh = jnp.where(keep, pltpu.roll(bv, shift, 0), 0.0)
            bv = av * b_sh + bv
            av = av * a_sh
        hv = av * h + bv
        h_ref[pl.ds(off, SUBLANES), :] = hv
        last = hv[0:1, :] if reverse else hv[SUBLANES - 1:SUBLANES, :]
        return jnp.broadcast_to(last, (SUBLANES, D_RNN))

    carry[...] = lax.fori_loop(0, ngroups, step, h0)


def _lru(proj, row0, nb, s, cw, cb, wa, ba, wx, bx, lam, reverse):
    tc = 256
    nchunks = s // tc
    t_all = proj.shape[0]
    blk0 = row0 // tc
    last8 = t_all // SUBLANES - 1

    def cidx(i):
        return (nchunks - 1 - i) if reverse else i

    def cur(b, i):
        return (blk0 + b * nchunks + cidx(i), COL_XRNN)

    def prev8(b, i):
        r = (blk0 + b * nchunks + cidx(i)) * (tc // SUBLANES) - 1
        return (jnp.maximum(r, 0), COL_XRNN)

    def next8(b, i):
        r = (blk0 + b * nchunks + cidx(i) + 1) * (tc // SUBLANES)
        return (jnp.minimum(r, last8), COL_XRNN)

    full = lambda shape: pl.BlockSpec(shape, lambda b, i: (0,) * len(shape))
    body = functools.partial(_lru_body, reverse=reverse, nchunks=nchunks, tc=tc)
    return pl.pallas_call(
        body,
        grid=(nb, nchunks),
        in_specs=[
            pl.BlockSpec((tc, D_RNN), cur),
            pl.BlockSpec((SUBLANES, D_RNN), prev8),
            pl.BlockSpec((SUBLANES, D_RNN), next8),
            full((CONV_W, D_RNN)), full((1, D_RNN)),
            full((LRU_BLOCKS, LRU_BW, LRU_BW)), full((1, D_RNN)),
            full((LRU_BLOCKS, LRU_BW, LRU_BW)), full((1, D_RNN)),
            full((1, D_RNN)),
        ],
        out_specs=pl.BlockSpec((tc, D_RNN), lambda b, i: (b * nchunks + cidx(i), 0)),
        out_shape=jax.ShapeDtypeStruct((nb * s, D_RNN), F32),
        scratch_shapes=[pltpu.VMEM((tc, D_RNN), F32), pltpu.VMEM((tc, D_RNN), F32),
                        pltpu.VMEM((SUBLANES, D_RNN), F32)],
        compiler_params=pltpu.CompilerParams(dimension_semantics=("arbitrary", "arbitrary")),
        name="lru_bwd" if reverse else "lru_fwd",
    )(proj, proj, proj, cw, cb, wa, ba, wx, bx, lam)


def _attn_body(far_ref, q_ref, k_ref, v_ref, bt_ref, lamv_ref, g_ref, o_ref,
               q1_s, q2_s, m1_s, l1_s, acc1_s, m2_s, l2_s, acc2_s, *, nkv):
    h = pl.program_id(1)
    i = pl.program_id(2)
    j = pl.program_id(3)

    @pl.when(j == 0)
    def _():
        q = q_ref[...].astype(F32) * (HEAD_DIM ** -0.5 * LOG2E)
        lane = lax.broadcasted_iota(jnp.int32, q.shape, 1)
        q1_s[...] = jnp.where(lane < HEAD_DIM, q, 0.0).astype(BF16)
        q2_s[...] = jnp.where(lane >= HEAD_DIM, q, 0.0).astype(BF16)
        for m_s, l_s, acc_s in ((m1_s, l1_s, acc1_s), (m2_s, l2_s, acc2_s)):
            m_s[...] = jnp.full(m_s.shape, -jnp.inf, F32)
            l_s[...] = jnp.zeros(l_s.shape, F32)
            acc_s[...] = jnp.zeros(acc_s.shape, F32)

    k = k_ref[...]
    vt = v_ref[...]

    def update(bias_tile, bias_const):
        for q_s, m_s, l_s, acc_s in ((q1_s, m1_s, l1_s, acc1_s), (q2_s, m2_s, l2_s, acc2_s)):
            s = lax.dot_general(k, q_s[...], NT_DIMS, preferred_element_type=F32)
            if bias_tile is not None:
                s = s + bias_tile
            m_old = m_s[...]
            m_new = jnp.maximum(m_old, jnp.max(s, axis=0, keepdims=True) + bias_const)
            alpha = jnp.exp2(m_old - m_new)
            p = jnp.exp2(s - (m_new - bias_const))
            l_s[...] = alpha * l_s[...] + jnp.sum(p, axis=0, keepdims=True)
            acc_s[...] = alpha * acc_s[...] + jnp.dot(vt, p.astype(BF16), preferred_element_type=F32)
            m_s[...] = m_new

    near = jnp.abs(j - i) <= 1

    @pl.when(near)
    def _():
        update(bt_ref[j - i + 1], 0.0)

    @pl.when(jnp.logical_not(near))
    def _():
        update(None, jnp.where(j < i, far_ref[0, h], far_ref[1, h]))

    @pl.when(j == nkv - 1)
    def _():
        lv = lamv_ref[...]
        lam = (jnp.exp(jnp.sum(lv[0:1] * lv[1:2], axis=-1, keepdims=True))
               - jnp.exp(jnp.sum(lv[2:3] * lv[3:4], axis=-1, keepdims=True)) + LAM_INIT)
        o = acc1_s[...] / l1_s[...] - lam * (acc2_s[...] / l2_s[...])
        o_ref[...] = _rms(o.T, g_ref[...]) * (1.0 - LAM_INIT)


def _attention(qk, vt, row0, nb, s, far, btiles, lamv, subln_g):
    tq = btiles.shape[-1]
    nq = s // tq
    blk0 = row0 // tq

    def qmap(b, h, i, j):
        return (0, h, blk0 + b * nq + i, 0)

    def kmap(b, h, i, j):
        return (1, h, blk0 + b * nq + j, 0)

    def vmap(b, h, i, j):
        return (h, 0, blk0 + b * nq + j)

    body = functools.partial(_attn_body, nkv=nq)
    stat = pltpu.VMEM((1, tq), F32)
    acc = pltpu.VMEM((V_DIM, tq), F32)
    return pl.pallas_call(
        body,
        grid=(nb, N_HEADS, nq, nq),
        in_specs=[
            pl.BlockSpec(memory_space=pltpu.SMEM),
            pl.BlockSpec((None, None, tq, V_DIM), qmap),
            pl.BlockSpec((None, None, tq, V_DIM), kmap),
            pl.BlockSpec((None, V_DIM, tq), vmap),
            pl.BlockSpec((None, 3, tq, tq), lambda b, h, i, j: (h, 0, 0, 0)),
            pl.BlockSpec((4, HEAD_DIM), lambda b, h, i, j: (0, 0)),
            pl.BlockSpec((1, V_DIM), lambda b, h, i, j: (0, 0)),
        ],
        out_specs=pl.BlockSpec((tq, V_DIM), lambda b, h, i, j: (b * nq + i, h)),
        out_shape=jax.ShapeDtypeStruct((nb * s, ATTN_W), F32),
        scratch_shapes=[pltpu.VMEM((tq, V_DIM), BF16), pltpu.VMEM((tq, V_DIM), BF16),
                        stat, stat, acc, stat, stat, acc],
        compiler_params=pltpu.CompilerParams(
            dimension_semantics=("parallel", "parallel", "parallel", "arbitrary"),
            vmem_limit_bytes=VMEM_LIMIT),
        name="diff_attn",
    )(far, qk, qk, vt, btiles, lamv, subln_g)


def _merge_body(x_ref, hf_ref, hb_ref, grnn_ref, attn_ref, glr_ref, gla_ref, bg_ref,
                wr_ref, wa_ref, wo_ref, n2_ref, wq_ref, x1_ref, xn2_ref, qp_ref):
    hg = ((hf_ref[...] + hb_ref[...]) * jax.nn.gelu(grnn_ref[...])).astype(BF16)
    y_rnn = jnp.dot(hg, wr_ref[...], preferred_element_type=F32)
    y_attn = jnp.dot(attn_ref[...].astype(BF16), wa_ref[...], preferred_element_type=F32)
    bg = bg_ref[...]
    g_r = jax.nn.sigmoid(glr_ref[...] + bg[0:1])
    g_a = jax.nn.sigmoid(gla_ref[...] + bg[1:2])
    merged = (g_r * y_rnn + g_a * y_attn).astype(BF16)
    x1 = x_ref[...] + jnp.dot(merged, wo_ref[...], preferred_element_type=F32)
    x1_ref[...] = x1
    xn2 = _rms(x1, n2_ref[...])
    xn2_ref[...] = xn2
    qp_ref[...] = jnp.dot(xn2.astype(BF16), wq_ref[...], preferred_element_type=F32)


def _merge(x2d, hf, hb, proj, attn, bg, wr, wa, wo, n2, wq):
    t = x2d.shape[0]
    tm = 256
    row = lambda c: pl.BlockSpec((tm, D_MODEL), lambda i: (i, c))
    full = lambda shape: pl.BlockSpec(shape, lambda i: (0,) * len(shape))
    qw = PEER_HEADS * D_KEY
    return pl.pallas_call(
        _merge_body,
        grid=(t // tm,),
        in_specs=[row(0), row(0), row(0), row(COL_GRNN), row(0), row(COL_GLR), row(COL_GLA),
                  full((2, D_MODEL)), full((D_RNN, D_MODEL)), full((ATTN_W, D_MODEL)),
                  full((D_MODEL, D_MODEL)), full((1, D_MODEL)), full((D_MODEL, qw))],
        out_specs=[row(0), row(0), pl.BlockSpec((tm, qw), lambda i: (i, 0))],
        out_shape=[jax.ShapeDtypeStruct((t, D_MODEL), F32), jax.ShapeDtypeStruct((t, D_MODEL), F32),
                   jax.ShapeDtypeStruct((t, qw), F32)],
        compiler_params=pltpu.CompilerParams(dimension_semantics=("parallel",),
                                             vmem_limit_bytes=VMEM_LIMIT),
        name="merge_proj",
    )(x2d, hf, hb, proj, attn, proj, proj, bg, wr, wa, wo, n2, wq)


def _topk_rows(s, k, ids=None):
    if ids is None:
        ids = lax.broadcasted_iota(jnp.int32, s.shape, 0).astype(F32)
    slot = lax.broadcasted_iota(jnp.int32, (k, s.shape[1]), 0)
    vals = jnp.zeros((k, s.shape[1]), F32)
    idxs = jnp.zeros((k, s.shape[1]), F32)
    for t in range(k):
        m = jnp.max(s, axis=0, keepdims=True)
        pos = jnp.min(jnp.where(s == m, ids, jnp.inf), axis=0, keepdims=True)
        vals = jnp.where(slot == t, m, vals)
        idxs = jnp.where(slot == t, pos, idxs)
        s = jnp.where(ids == pos, -jnp.inf, s)
    return vals, idxs


_PAIRS = [(a, b) for a in range(PEER_TOPK) for b in range(PEER_TOPK) if (a + 1) * (b + 1) <= PEER_TOPK]


def _rows_from(src, picks, fill):
    if all(p is not None for p in picks) and picks[0] % SUBLANES == 0 and picks == list(range(picks[0], picks[0] + SUBLANES)):
        return src[picks[0]:picks[0] + SUBLANES, :]
    row = lax.broadcasted_iota(jnp.int32, (SUBLANES, src.shape[1]), 0)
    out = jnp.full((SUBLANES, src.shape[1]), fill, F32)
    for r, p in enumerate(picks):
        if p is not None:
            out = jnp.where(row == r, src[p:p + 1, :], out)
    return out


def _pair_candidates(sv1, sv2):
    pairs = _PAIRS + [None] * (-len(_PAIRS) % SUBLANES)
    width = sv1.shape[1]
    row = lax.broadcasted_iota(jnp.int32, (SUBLANES, width), 0)
    sums, ids = [], []
    for g0 in range(0, len(pairs), SUBLANES):
        grp = pairs[g0:g0 + SUBLANES]
        a_rows = _rows_from(sv1, [None if pr is None else pr[0] for pr in grp], -jnp.inf)
        b_rows = _rows_from(sv2, [None if pr is None else pr[1] for pr in grp], 0.0)
        sums.append(a_rows + b_rows)
        idv = jnp.full((SUBLANES, width), float(PEER_TOPK * PEER_TOPK), F32)
        for r, pr in enumerate(grp):
            if pr is not None:
                idv = jnp.where(row == r, float(pr[0] * PEER_TOPK + pr[1]), idv)
        ids.append(idv)
    return jnp.concatenate(sums, axis=0), jnp.concatenate(ids, axis=0)


def _router_body(qp_ref, keys_ref, idx_ref, g_ref):
    slot = lax.broadcasted_iota(jnp.int32, (PEER_TOPK, qp_ref.shape[0]), 0)
    for h in range(PEER_HEADS):
        sv, si = [], []
        for p in range(2):
            c0 = (h * 2 + p) * D_HALF
            q = qp_ref[:, c0:c0 + D_HALF].astype(BF16)
            s = lax.dot_general(keys_ref[h, p], q, (((1,), (1,)), ((), ())), preferred_element_type=F32)
            v, ix = _topk_rows(s, PEER_TOPK)
            sv.append(v)
            si.append(ix)
        cand, cand_id = _pair_candidates(sv[0], sv[1])
        top_s, pos = _topk_rows(cand, PEER_TOPK, cand_id)
        pos = pos.astype(jnp.int32)
        pa = pos >> 4
        pb = pos & (PEER_TOPK - 1)
        idx = jnp.zeros(slot.shape, F32)
        for t in range(PEER_TOPK):
            i1 = jnp.sum(jnp.where(slot == pa[t:t + 1, :], si[0], 0.0), axis=0, keepdims=True)
            i2 = jnp.sum(jnp.where(slot == pb[t:t + 1, :], si[1], 0.0), axis=0, keepdims=True)
            idx = jnp.where(slot == t, i1 * N_KEYS + i2, idx)
        idx = idx.astype(jnp.int32)
        e = jnp.exp(top_s - jnp.max(top_s, axis=0, keepdims=True))
        g = e / jnp.sum(e, axis=0, keepdims=True)
        idx_ref[h * PEER_TOPK:(h + 1) * PEER_TOPK, :] = idx
        g_ref[h * PEER_TOPK:(h + 1) * PEER_TOPK, :] = g


def _router(qp, keys):
    t = qp.shape[0]
    c = 256
    qw = PEER_HEADS * D_KEY
    return pl.pallas_call(
        _router_body,
        grid=(t // c,),
        in_specs=[pl.BlockSpec((c, qw), lambda i: (i, 0)),
                  pl.BlockSpec((PEER_HEADS, 2, N_KEYS, D_HALF), lambda i: (0, 0, 0, 0))],
        out_specs=[pl.BlockSpec((N_SEL, c), lambda i: (0, i)), pl.BlockSpec((N_SEL, c), lambda i: (0, i))],
        out_shape=[jax.ShapeDtypeStruct((N_SEL, t), jnp.int32), jax.ShapeDtypeStruct((N_SEL, t), F32)],
        compiler_params=pltpu.CompilerParams(dimension_semantics=("parallel",)),
        name="peer_router",
    )(qp, keys)


def _pack_body(t_ref, o_ref):
    o_ref[...] = pltpu.bitcast(t_ref[...].astype(BF16), jnp.int32)


def _pack_table(tab3):
    n = tab3.shape[0]
    r = 512
    return pl.pallas_call(
        _pack_body,
        grid=(n // r,),
        in_specs=[pl.BlockSpec((r, SUBLANES, LANES), lambda i: (i, 0, 0))],
        out_specs=pl.BlockSpec((r, ROW_SUB, LANES), lambda i: (i, 0, 0)),
        out_shape=jax.ShapeDtypeStruct((n, ROW_SUB, LANES), jnp.int32),
        compiler_params=pltpu.CompilerParams(dimension_semantics=("parallel",)),
        name="peer_pack",
    )(tab3)


PEER_UNROLL = SUBLANES


def _load_table(tab_hbm, tab_s, sem):
    @pl.when(pl.program_id(0) == 0)
    def _():
        cp = pltpu.make_async_copy(tab_hbm, tab_s, sem)
        cp.start()
        cp.wait()


def _gather_rows(tab_s, idx_ref, t, m_ref):
    for j in range(N_SEL):
        m_ref[j * ROW_SUB:(j + 1) * ROW_SUB, :] = tab_s[idx_ref[t, j]]


def _split3_bf16(x):
    hi = x.astype(BF16).astype(F32)
    r1 = x - hi
    mid = r1.astype(BF16).astype(F32)
    lo = r1 - mid
    return jnp.concatenate([hi, mid, lo, jnp.zeros_like(x)], axis=0).astype(BF16)


def _sum3(y):
    return y[0:SUBLANES] + y[SUBLANES:2 * SUBLANES] + y[2 * SUBLANES:3 * SUBLANES]


def _diag_mask():
    p = lax.broadcasted_iota(jnp.int32, (SUBLANES, N_SEL * SUBLANES), 0)
    c = lax.broadcasted_iota(jnp.int32, (SUBLANES, N_SEL * SUBLANES), 1)
    return (c & (SUBLANES - 1)) == p


def _peer_u_body(idx_ref, x_ref, g_ref, tab_hbm, w_ref, tab_s, m_s, fold_s, s_s, sem, *, tb):
    _load_table(tab_hbm, tab_s, sem)

    @pl.when(pl.program_id(0) == 0)
    def _():
        r = lax.broadcasted_iota(jnp.int32, fold_s.shape, 0)
        c = lax.broadcasted_iota(jnp.int32, fold_s.shape, 1)
        fold_s[...] = jnp.where((r >> 3) == c, 1.0, 0.0).astype(BF16)

    diag = _diag_mask()

    def group(gi, carry):
        base = gi * PEER_UNROLL
        for u in range(PEER_UNROLL):
            _gather_rows(tab_s, idx_ref, base + u, m_s.at[u])
        for u in range(PEER_UNROLL):
            rows = pltpu.bitcast(m_s[u], BF16)
            p = _sum3(lax.dot_general(_split3_bf16(x_ref[base + u]), rows, NT_DIMS, preferred_element_type=F32))
            p = jnp.where(diag, p, 0.0)
            s_s[base + u] = _sum3(jnp.dot(_split3_bf16(p), fold_s[...], preferred_element_type=F32))
        return carry

    lax.fori_loop(0, tb // PEER_UNROLL, group, 0)
    s = s_s[:, 0, :]
    for q in range(1, SUBLANES):
        s = s + s_s[:, q, :]
    w_ref[...] = g_ref[...] * jax.nn.gelu(s)


def _peer_u(idx_t, x3, g_t, tab):
    t = idx_t.shape[0]
    tb = 128
    body = functools.partial(_peer_u_body, tb=tb)
    return pl.pallas_call(
        body,
        grid=(t // tb,),
        in_specs=[pl.BlockSpec((tb, N_SEL), lambda i: (i, 0), memory_space=pltpu.SMEM),
                  pl.BlockSpec((tb, SUBLANES, LANES), lambda i: (i, 0, 0)),
                  pl.BlockSpec((tb, N_SEL), lambda i: (i, 0)),
                  pl.BlockSpec(memory_space=pl.ANY)],
        out_specs=pl.BlockSpec((tb, N_SEL), lambda i: (i, 0)),
        out_shape=jax.ShapeDtypeStruct((t, N_SEL), F32),
        scratch_shapes=[pltpu.VMEM((N_EXPERTS, ROW_SUB, LANES), jnp.int32),
                        pltpu.VMEM((PEER_UNROLL, N_SEL * ROW_SUB, LANES), jnp.int32),
                        pltpu.VMEM((N_SEL * SUBLANES, N_SEL), BF16),
                        pltpu.VMEM((tb, SUBLANES, LANES), F32),
                        pltpu.SemaphoreType.DMA(())],
        compiler_params=pltpu.CompilerParams(dimension_semantics=("arbitrary",),
                                             vmem_limit_bytes=VMEM_LIMIT),
        name="peer_u",
    )(idx_t, x3, g_t, tab)


def _peer_v_body(idx_ref, w_ref, tab_hbm, o_ref, tab_s, m_s, spread_s, sem, *, tb):
    _load_table(tab_hbm, tab_s, sem)

    @pl.when(pl.program_id(0) == 0)
    def _():
        r = lax.broadcasted_iota(jnp.int32, spread_s.shape, 0)
        c = lax.broadcasted_iota(jnp.int32, spread_s.shape, 1)
        spread_s[...] = jnp.where((c >> 3) == r, 1.0, 0.0).astype(BF16)

    diag = _diag_mask()
    zero = jnp.zeros((SUBLANES, N_SEL * SUBLANES), F32)

    def group(gi, carry):
        base = pl.multiple_of(gi * PEER_UNROLL, PEER_UNROLL)
        for u in range(PEER_UNROLL):
            _gather_rows(tab_s, idx_ref, base + u, m_s.at[u])
        e = jnp.dot(_split3_bf16(w_ref[pl.ds(base, PEER_UNROLL), :]), spread_s[...], preferred_element_type=F32)
        for u in range(PEER_UNROLL):
            rows = pltpu.bitcast(m_s[u], BF16)
            terms = [jnp.where(diag, jnp.broadcast_to(e[SUBLANES * k + u:SUBLANES * k + u + 1, :], diag.shape), 0.0)
                     for k in range(3)]
            lhs = jnp.concatenate(terms + [zero], axis=0).astype(BF16)
            o_ref[base + u] = _sum3(jnp.dot(lhs, rows, preferred_element_type=F32))
        return carry

    lax.fori_loop(0, tb // PEER_UNROLL, group, 0)


def _peer_v(idx_t, w, tab):
    t = idx_t.shape[0]
    tb = 128
    body = functools.partial(_peer_v_body, tb=tb)
    return pl.pallas_call(
        body,
        grid=(t // tb,),
        in_specs=[pl.BlockSpec((tb, N_SEL), lambda i: (i, 0), memory_space=pltpu.SMEM),
                  pl.BlockSpec((tb, N_SEL), lambda i: (i, 0)),
                  pl.BlockSpec(memory_space=pl.ANY)],
        out_specs=pl.BlockSpec((tb, SUBLANES, LANES), lambda i: (i, 0, 0)),
        out_shape=jax.ShapeDtypeStruct((t, SUBLANES, LANES), F32),
        scratch_shapes=[pltpu.VMEM((N_EXPERTS, ROW_SUB, LANES), jnp.int32),
                        pltpu.VMEM((PEER_UNROLL, N_SEL * ROW_SUB, LANES), jnp.int32),
                        pltpu.VMEM((N_SEL, N_SEL * SUBLANES), BF16),
                        pltpu.SemaphoreType.DMA(())],
        compiler_params=pltpu.CompilerParams(dimension_semantics=("arbitrary",),
                                             vmem_limit_bytes=VMEM_LIMIT),
        name="peer_v",
    )(idx_t, w, tab)


def _final_body(x1_ref, o_ref, g_ref, y_ref):
    y_ref[...] = _rms(x1_ref[...] + o_ref[...], g_ref[...])


def _final(x1, o, g):
    t = x1.shape[0]
    tm = 512
    row = pl.BlockSpec((tm, D_MODEL), lambda i: (i, 0))
    return pl.pallas_call(
        _final_body,
        grid=(t // tm,),
        in_specs=[row, row, pl.BlockSpec((1, D_MODEL), lambda i: (0, 0))],
        out_specs=row,
        out_shape=jax.ShapeDtypeStruct((t, D_MODEL), F32),
        compiler_params=pltpu.CompilerParams(dimension_semantics=("parallel",)),
        name="final_norm",
    )(x1, o, g)


def _rel_bucket(rel):
    nb = NUM_BUCKETS // 2
    ret = jnp.where(rel > 0, nb, 0).astype(jnp.int32)
    n = jnp.abs(rel)
    max_exact = nb // 2
    nf = jnp.maximum(n, 1).astype(F32)
    large = max_exact + (jnp.log(nf / max_exact) / math.log(MAX_DISTANCE / max_exact) * (nb - max_exact)).astype(jnp.int32)
    large = jnp.minimum(large, nb - 1)
    return ret + jnp.where(n < max_exact, n, large)


def _bias_tables(rel_bias, tq):
    assert tq >= MAX_DISTANCE
    m = jnp.arange(2 * tq - 1, dtype=jnp.int32)
    rel = jnp.stack([d * tq + m - (tq - 1) for d in (-1, 0, 1)])
    w = jnp.transpose(rel_bias[_rel_bucket(rel)], (2, 0, 1)).astype(F32) * LOG2E
    period = 2 * tq
    v = jnp.concatenate([w[..., tq - 1::-1], jnp.zeros(w.shape[:2] + (1,), F32), w[..., :tq - 1:-1]], axis=-1)
    rep = jnp.tile(v, (1, 1, tq))[..., :tq * (period - 1)]
    tiles = rep.reshape(w.shape[:2] + (tq, period - 1))[..., :tq]
    far = rel_bias[_rel_bucket(jnp.array([-MAX_DISTANCE, MAX_DISTANCE], jnp.int32))].astype(F32) * LOG2E
    return tiles, far


def kernel(x_prompt, x_sample, norm1_g, w_in, b_gate, conv_w, conv_b, lru_wa_f, lru_ba_f, lru_wx_f, lru_bx_f, lru_lam_f, lru_wa_b, lru_ba_b, lru_wx_b, lru_bx_b, lru_lam_b, lam_q1, lam_k1, lam_q2, lam_k2, subln_g, rel_bias, w_rnn_out, w_attn_out, w_out, norm2_g, peer_wq, peer_keys, peer_u, peer_v, final_g):
    groups = [x_prompt.shape[:2], x_sample.shape[:2]]
    x2d = jnp.concatenate([x_prompt.reshape(-1, D_MODEL), x_sample.reshape(-1, D_MODEL)], axis=0)
    t = x2d.shape[0]
    l = 0
    row = lambda a: a.reshape(1, -1)

    w_main = jnp.concatenate([w_in[l][:, :QKV_COL0], w_in[l][:, QKV_COL1:]], axis=1).astype(BF16)
    proj = _inproj(x2d, row(norm1_g[l]), w_main)
    qk, vt = _qkv(x2d, row(norm1_g[l]), w_in[l][:, QKV_COL0:QKV_COL1].astype(BF16))

    btiles, far = _bias_tables(rel_bias, 512)
    lamv = jnp.stack([lam_q1[l], lam_k1[l], lam_q2[l], lam_k2[l]])
    lru_f = (conv_w[l], row(conv_b[l]), lru_wa_f[l].astype(BF16), row(lru_ba_f[l]),
             lru_wx_f[l].astype(BF16), row(lru_bx_f[l]), row(lru_lam_f[l]))
    lru_b = (conv_w[l], row(conv_b[l]), lru_wa_b[l].astype(BF16), row(lru_ba_b[l]),
             lru_wx_b[l].astype(BF16), row(lru_bx_b[l]), row(lru_lam_b[l]))
    hf, hb, attn = [], [], []
    row0 = 0
    for nb, s in groups:
        hf.append(_lru(proj, row0, nb, s, *lru_f, reverse=False))
        hb.append(_lru(proj, row0, nb, s, *lru_b, reverse=True))
        attn.append(_attention(qk, vt, row0, nb, s, far, btiles, lamv, row(subln_g[l])))
        row0 += nb * s
    hf, hb, attn = (jnp.concatenate(a, axis=0) for a in (hf, hb, attn))

    x1, xn2, qp = _merge(x2d, hf, hb, proj, attn, b_gate[l].reshape(2, D_MODEL),
                         w_rnn_out[l].astype(BF16), w_attn_out[l].astype(BF16), w_out[l].astype(BF16),
                         row(norm2_g[l]), peer_wq[l].astype(BF16))

    idx, g = _router(qp, peer_keys[l].astype(BF16))
    idx_t = idx.T
    g_t = g.T
    tab_u = _pack_table(peer_u[l].reshape(N_EXPERTS, SUBLANES, LANES))
    tab_v = _pack_table(peer_v[l].reshape(N_EXPERTS, SUBLANES, LANES))
    w = _peer_u(idx_t, xn2.reshape(t, SUBLANES, LANES), g_t, tab_u)
    o = _peer_v(idx_t, w, tab_v).reshape(t, D_MODEL)

    y = _final(x1, o, row(final_g))
    n0 = groups[0][0] * groups[0][1]
    return (y[:n0].reshape(x_prompt.shape), y[n0:].reshape(x_sample.shape))
```

```python
import functools
import math

import jax
import jax.numpy as jnp
from jax import lax
from jax.experimental import pallas as pl
from jax.experimental.pallas import tpu as pltpu

F32 = jnp.float32
BF16 = jnp.bfloat16

D_MODEL = 1024
D_RNN = 1024
LRU_BLOCKS = 4
LRU_BW = D_RNN // LRU_BLOCKS
LRU_C = 8.0
CONV_W = 4
N_HEADS = 8
HEAD_DIM = 64
V_DIM = 2 * HEAD_DIM
V_AUG = V_DIM + 16
ATTN_W = N_HEADS * V_DIM
NUM_BUCKETS = 32
MAX_DISTANCE = 128
PEER_HEADS = 8
N_KEYS = 128
N_EXPERTS = N_KEYS * N_KEYS
PEER_TOPK = 16
D_KEY = 256
D_HALF = D_KEY // 2
N_SEL = PEER_HEADS * PEER_TOPK
IN_W = 2 * D_RNN + 3 * ATTN_W + 2 * D_MODEL
EPS = 1e-6
LAM_INIT = 0.8 - 0.6 * math.exp(-0.3 * 0)
LOG2E = math.log2(math.e)
NT_DIMS = (((1,), (1,)), ((), ()))
BIAS_REACH = 2

COL_XRNN, COL_GRNN, COL_GLR, COL_GLA = range(4)
QKV_COL0 = 2 * D_RNN
QKV_COL1 = QKV_COL0 + 3 * ATTN_W

SUBLANES = 8
LANES = 128
ROW_WORDS = D_MODEL // 2
ROW_SUB = ROW_WORDS // LANES
VMEM_LIMIT = 56 * 1024 * 1024


def _rms(x, g):
    return x * lax.rsqrt(jnp.mean(x * x, axis=-1, keepdims=True) + EPS) * g


def _inproj_body(x_ref, g_ref, w_ref, o_ref, xn_ref):
    @pl.when(pl.program_id(1) == 0)
    def _():
        xn_ref[...] = _rms(x_ref[...], g_ref[...]).astype(BF16)

    o_ref[...] = jnp.dot(xn_ref[...], w_ref[...], preferred_element_type=F32)


def _inproj(x2d, g, w):
    t = x2d.shape[0]
    n_out = w.shape[1]
    tm, tn = 512, 1024
    return pl.pallas_call(
        _inproj_body,
        grid=(t // tm, n_out // tn),
        in_specs=[
            pl.BlockSpec((tm, D_MODEL), lambda i, j: (i, 0)),
            pl.BlockSpec((1, D_MODEL), lambda i, j: (0, 0)),
            pl.BlockSpec((D_MODEL, tn), lambda i, j: (0, j)),
        ],
        out_specs=pl.BlockSpec((tm, tn), lambda i, j: (i, j)),
        out_shape=jax.ShapeDtypeStruct((t, n_out), F32),
        scratch_shapes=[pltpu.VMEM((tm, D_MODEL), BF16)],
        compiler_params=pltpu.CompilerParams(dimension_semantics=("parallel", "arbitrary")),
        name="inproj",
    )(x2d, g, w)


def _qkv_body(x_ref, g_ref, w_ref, qk_ref, vt_ref, xn_ref):
    j = pl.program_id(1)

    @pl.when(j == 0)
    def _():
        xn_ref[...] = _rms(x_ref[...], g_ref[...]).astype(BF16)

    res = jnp.dot(xn_ref[...], w_ref[...], preferred_element_type=F32)

    @pl.when(j < 2)
    def _():
        for h in range(N_HEADS):
            qk_ref[h] = res[:, h * V_DIM:(h + 1) * V_DIM].astype(BF16)

    @pl.when(j == 2)
    def _():
        for h in range(N_HEADS):
            vt_ref[h, 0:V_DIM, :] = res[:, h * V_DIM:(h + 1) * V_DIM].T.astype(BF16)
            vt_ref[h, V_DIM:V_AUG, :] = jnp.ones((V_AUG - V_DIM, res.shape[0]), BF16)


def _qkv(x2d, g, w):
    t = x2d.shape[0]
    tm = 512
    return pl.pallas_call(
        _qkv_body,
        grid=(t // tm, 3),
        in_specs=[
            pl.BlockSpec((tm, D_MODEL), lambda i, j: (i, 0)),
            pl.BlockSpec((1, D_MODEL), lambda i, j: (0, 0)),
            pl.BlockSpec((D_MODEL, ATTN_W), lambda i, j: (0, j)),
        ],
        out_specs=[pl.BlockSpec((None, N_HEADS, tm, V_DIM), lambda i, j: (jnp.minimum(j, 1), 0, i, 0)),
                   pl.BlockSpec((N_HEADS, V_AUG, tm), lambda i, j: (0, 0, i))],
        out_shape=[jax.ShapeDtypeStruct((2, N_HEADS, t, V_DIM), BF16),
                   jax.ShapeDtypeStruct((N_HEADS, V_AUG, t), BF16)],
        scratch_shapes=[pltpu.VMEM((tm, D_MODEL), BF16)],
        compiler_params=pltpu.CompilerParams(dimension_semantics=("parallel", "arbitrary")),
        name="qkv_proj",
    )(x2d, g, w)


def _lru_body(xc_ref, xp_ref, xn_ref, cw_ref, cb_ref, wa_ref, ba_ref, wx_ref, bx_ref, lam_ref,
              h_ref, a_s, b_s, carry, *, reverse, nchunks, tc):
    i = pl.program_id(1)
    c = (nchunks - 1 - i) if reverse else i
    x = xc_ref[...]
    rows = lax.broadcasted_iota(jnp.int32, (tc, D_RNN), 0)
    prev = jnp.where(c > 0, xp_ref[SUBLANES - 1:SUBLANES, :], 0.0)
    nxt0 = jnp.where(c < nchunks - 1, xn_ref[0:1, :], 0.0)
    nxt1 = jnp.where(c < nchunks - 1, xn_ref[1:2, :], 0.0)
    xm1 = jnp.where(rows == 0, prev, pltpu.roll(x, 1, 0))
    xp1 = jnp.where(rows == tc - 1, nxt0, pltpu.roll(x, tc - 1, 0))
    xp2 = jnp.where(rows == tc - 2, nxt0, jnp.where(rows == tc - 1, nxt1, pltpu.roll(x, tc - 2, 0)))
    cw = cw_ref[...]
    xc = cb_ref[...] + xm1 * cw[0:1] + x * cw[1:2] + xp1 * cw[2:3] + xp2 * cw[3:4]

    xcb = xc.astype(BF16)
    r_parts, i_parts = [], []
    for n in range(LRU_BLOCKS):
        xs = xcb[:, n * LRU_BW:(n + 1) * LRU_BW]
        r_parts.append(jnp.dot(xs, wa_ref[n], preferred_element_type=F32))
        i_parts.append(jnp.dot(xs, wx_ref[n], preferred_element_type=F32))
    r = jax.nn.sigmoid(jnp.concatenate(r_parts, axis=1) + ba_ref[...])
    ig = jax.nn.sigmoid(jnp.concatenate(i_parts, axis=1) + bx_ref[...])
    z = -lam_ref[...]
    softplus = jnp.maximum(z, 0.0) + jnp.log1p(jnp.exp(-jnp.abs(z)))
    log_a = -LRU_C * r * softplus
    a = jnp.exp(log_a)
    mult = jnp.sqrt(-jnp.tanh(log_a) * (a * a + 1.0))
    edge_row = tc - 1 if reverse else 0
    edge_chunk = nchunks - 1 if reverse else 0
    mult = jnp.where(rows == jnp.where(c == edge_chunk, edge_row, -1), 1.0, mult)
    a_s[...] = a
    b_s[...] = mult * ig * xc

    row8 = lax.broadcasted_iota(jnp.int32, (SUBLANES, D_RNN), 0)
    ngroups = tc // SUBLANES
    h0 = jnp.where(i == 0, 0.0, carry[...])

    def step(gi, h):
        g = (ngroups - 1 - gi) if reverse else gi
        off = pl.multiple_of(g * SUBLANES, SUBLANES)
        av = a_s[pl.ds(off, SUBLANES), :]
        bv = b_s[pl.ds(off, SUBLANES), :]
        for s in (1, 2, 4):
            if reverse:
                keep = row8 < SUBLANES - s
                shift = SUBLANES - s
            else:
                keep = row8 >= s
                shift = s
            a_sh = jnp.where(keep, pltpu.roll(av, shift, 0), 1.0)
            b_sh = jnp.where(keep, pltpu.roll(bv, shift, 0), 0.0)
            bv = av * b_sh + bv
            av = av * a_sh
        hv = av * h + bv
        h_ref[pl.ds(off, SUBLANES), :] = hv
        last = hv[0:1, :] if reverse else hv[SUBLANES - 1:SUBLANES, :]
        return jnp.broadcast_to(last, (SUBLANES, D_RNN))

    carry[...] = lax.fori_loop(0, ngroups, step, h0)


def _lru(proj, row0, nb, s, cw, cb, wa, ba, wx, bx, lam, reverse):
    tc = 256
    nchunks = s // tc
    t_all = proj.shape[0]
    blk0 = row0 // tc
    last8 = t_all // SUBLANES - 1

    def cidx(i):
        return (nchunks - 1 - i) if reverse else i

    def cur(b, i):
        return (blk0 + b * nchunks + cidx(i), COL_XRNN)

    def prev8(b, i):
        r = (blk0 + b * nchunks + cidx(i)) * (tc // SUBLANES) - 1
        return (jnp.maximum(r, 0), COL_XRNN)

    def next8(b, i):
        r = (blk0 + b * nchunks + cidx(i) + 1) * (tc // SUBLANES)
        return (jnp.minimum(r, last8), COL_XRNN)

    full = lambda shape: pl.BlockSpec(shape, lambda b, i: (0,) * len(shape))
    body = functools.partial(_lru_body, reverse=reverse, nchunks=nchunks, tc=tc)
    return pl.pallas_call(
        body,
        grid=(nb, nchunks),
        in_specs=[
            pl.BlockSpec((tc, D_RNN), cur),
            pl.BlockSpec((SUBLANES, D_RNN), prev8),
            pl.BlockSpec((SUBLANES, D_RNN), next8),
            full((CONV_W, D_RNN)), full((1, D_RNN)),
            full((LRU_BLOCKS, LRU_BW, LRU_BW)), full((1, D_RNN)),
            full((LRU_BLOCKS, LRU_BW, LRU_BW)), full((1, D_RNN)),
            full((1, D_RNN)),
        ],
        out_specs=pl.BlockSpec((tc, D_RNN), lambda b, i: (b * nchunks + cidx(i), 0)),
        out_shape=jax.ShapeDtypeStruct((nb * s, D_RNN), F32),
        scratch_shapes=[pltpu.VMEM((tc, D_RNN), F32), pltpu.VMEM((tc, D_RNN), F32),
                        pltpu.VMEM((SUBLANES, D_RNN), F32)],
        compiler_params=pltpu.CompilerParams(dimension_semantics=("arbitrary", "arbitrary")),
        name="lru_bwd" if reverse else "lru_fwd",
    )(proj, proj, proj, cw, cb, wa, ba, wx, bx, lam)


def _attn_body(q_ref, k_ref, vt_ref, bt_ref, lamv_ref, g_ref, o_ref,
               q1_s, q2_s, m1_s, acc1_s, m2_s, acc2_s, sa_s, sb_s, *, nkv, tkv):
    i = pl.program_id(2)

    q = q_ref[...].astype(F32) * (HEAD_DIM ** -0.5 * LOG2E)
    lane = lax.broadcasted_iota(jnp.int32, q.shape, 1)
    q1_s[...] = jnp.where(lane < HEAD_DIM, q, 0.0).astype(BF16)
    q2_s[...] = jnp.where(lane >= HEAD_DIM, q, 0.0).astype(BF16)
    for m_s, acc_s in ((m1_s, acc1_s), (m2_s, acc2_s)):
        m_s[...] = jnp.full(m_s.shape, -jnp.inf, F32)
        acc_s[...] = jnp.zeros(acc_s.shape, F32)

    def scores(j, buf):
        off = pl.multiple_of(j * tkv, tkv)
        k = k_ref[pl.ds(off, tkv), :]
        bias = bt_ref[jnp.clip(j - i, -BIAS_REACH, BIAS_REACH) + BIAS_REACH]
        for c, q_s in enumerate((q1_s, q2_s)):
            buf[c] = lax.dot_general(k, q_s[...], NT_DIMS, preferred_element_type=F32) + bias

    def accumulate(j, buf):
        off = pl.multiple_of(j * tkv, tkv)
        vt = vt_ref[:, pl.ds(off, tkv)]
        for c, (m_s, acc_s) in enumerate(((m1_s, acc1_s), (m2_s, acc2_s))):
            s = buf[c]
            m_old = m_s[...]
            m_new = jnp.maximum(m_old, jnp.max(s, axis=0, keepdims=True))
            alpha = jnp.exp2(m_old - m_new)
            p = jnp.exp2((s - m_new).astype(BF16))
            acc_s[...] = alpha * acc_s[...] + jnp.dot(vt, p, preferred_element_type=F32)
            m_s[...] = m_new

    scores(0, sa_s)

    def pair(u, carry):
        j = 2 * u
        scores(j + 1, sb_s)
        accumulate(j, sa_s)
        scores(j + 2, sa_s)
        accumulate(j + 1, sb_s)
        return carry

    lax.fori_loop(0, nkv // 2 - 1, pair, 0)
    scores(nkv - 1, sb_s)
    accumulate(nkv - 2, sa_s)
    accumulate(nkv - 1, sb_s)

    lv = lamv_ref[...]
    lam = (jnp.exp(jnp.sum(lv[0:1] * lv[1:2], axis=-1, keepdims=True))
           - jnp.exp(jnp.sum(lv[2:3] * lv[3:4], axis=-1, keepdims=True)) + LAM_INIT)
    o = (acc1_s[0:V_DIM, :] / acc1_s[V_DIM:V_DIM + 1, :]
         - lam * (acc2_s[0:V_DIM, :] / acc2_s[V_DIM:V_DIM + 1, :]))
    o_ref[...] = _rms(o.T, g_ref[...]) * (1.0 - LAM_INIT)


def _attention(qk, vt, row0, nb, s, btiles, lamv, subln_g):
    tq = btiles.shape[-1]
    nq = s // tq
    assert nq >= 2 and nq % 2 == 0
    blk0 = row0 // tq
    seq0 = row0 // s

    body = functools.partial(_attn_body, nkv=nq, tkv=tq)
    stat = pltpu.VMEM((1, tq), F32)
    acc = pltpu.VMEM((V_AUG, tq), F32)
    sbuf = pltpu.VMEM((2, tq, tq), F32)
    return pl.pallas_call(
        body,
        grid=(nb, N_HEADS, nq),
        in_specs=[
            pl.BlockSpec((None, None, tq, V_DIM), lambda b, h, i: (0, h, blk0 + b * nq + i, 0)),
            pl.BlockSpec((None, None, s, V_DIM), lambda b, h, i: (1, h, seq0 + b, 0)),
            pl.BlockSpec((None, V_AUG, s), lambda b, h, i: (h, 0, seq0 + b)),
            pl.BlockSpec((None, 2 * BIAS_REACH + 1, tq, tq), lambda b, h, i: (h, 0, 0, 0)),
            pl.BlockSpec((4, HEAD_DIM), lambda b, h, i: (0, 0)),
            pl.BlockSpec((1, V_DIM), lambda b, h, i: (0, 0)),
        ],
        out_specs=pl.BlockSpec((tq, V_DIM), lambda b, h, i: (b * nq + i, h)),
        out_shape=jax.ShapeDtypeStruct((nb * s, ATTN_W), F32),
        scratch_shapes=[pltpu.VMEM((tq, V_DIM), BF16), pltpu.VMEM((tq, V_DIM), BF16),
                        stat, acc, stat, acc, sbuf, sbuf],
        compiler_params=pltpu.CompilerParams(
            dimension_semantics=("parallel", "parallel", "arbitrary"),
            vmem_limit_bytes=VMEM_LIMIT),
        name="diff_attn",
    )(qk, qk, vt, btiles, lamv, subln_g)


def _merge_body(x_ref, hf_ref, hb_ref, grnn_ref, attn_ref, glr_ref, gla_ref, bg_ref,
                wr_ref, wa_ref, wo_ref, n2_ref, wq_ref, x1_ref, xn2_ref, qp_ref):
    hg = ((hf_ref[...] + hb_ref[...]) * jax.nn.gelu(grnn_ref[...])).astype(BF16)
    y_rnn = jnp.dot(hg, wr_ref[...], preferred_element_type=F32)
    y_attn = jnp.dot(attn_ref[...].astype(BF16), wa_ref[...], preferred_element_type=F32)
    bg = bg_ref[...]
    g_r = jax.nn.sigmoid(glr_ref[...] + bg[0:1])
    g_a = jax.nn.sigmoid(gla_ref[...] + bg[1:2])
    merged = (g_r * y_rnn + g_a * y_attn).astype(BF16)
    x1 = x_ref[...] + jnp.dot(merged, wo_ref[...], preferred_element_type=F32)
    x1_ref[...] = x1
    xn2 = _rms(x1, n2_ref[...])
    xn2_ref[...] = xn2
    qp_ref[...] = jnp.dot(xn2.astype(BF16), wq_ref[...], preferred_element_type=F32)


def _merge(x2d, hf, hb, proj, attn, bg, wr, wa, wo, n2, wq):
    t = x2d.shape[0]
    tm = 256
    row = lambda c: pl.BlockSpec((tm, D_MODEL), lambda i: (i, c))
    full = lambda shape: pl.BlockSpec(shape, lambda i: (0,) * len(shape))
    qw = PEER_HEADS * D_KEY
    return pl.pallas_call(
        _merge_body,
        grid=(t // tm,),
        in_specs=[row(0), row(0), row(0), row(COL_GRNN), row(0), row(COL_GLR), row(COL_GLA),
                  full((2, D_MODEL)), full((D_RNN, D_MODEL)), full((ATTN_W, D_MODEL)),
                  full((D_MODEL, D_MODEL)), full((1, D_MODEL)), full((D_MODEL, qw))],
        out_specs=[row(0), row(0), pl.BlockSpec((tm, qw), lambda i: (i, 0))],
        out_shape=[jax.ShapeDtypeStruct((t, D_MODEL), F32), jax.ShapeDtypeStruct((t, D_MODEL), F32),
                   jax.ShapeDtypeStruct((t, qw), F32)],
        compiler_params=pltpu.CompilerParams(dimension_semantics=("parallel",),
                                             vmem_limit_bytes=VMEM_LIMIT),
        name="merge_proj",
    )(x2d, hf, hb, proj, attn, proj, proj, bg, wr, wa, wo, n2, wq)


def _topk_rows(s, k, ids=None):
    if ids is None:
        ids = lax.broadcasted_iota(jnp.int32, s.shape, 0).astype(F32)
    slot = lax.broadcasted_iota(jnp.int32, (k, s.shape[1]), 0)
    vals = jnp.zeros((k, s.shape[1]), F32)
    idxs = jnp.zeros((k, s.shape[1]), F32)
    for t in range(k):
        m = jnp.max(s, axis=0, keepdims=True)
        pos = jnp.min(jnp.where(s == m, ids, jnp.inf), axis=0, keepdims=True)
        vals = jnp.where(slot == t, m, vals)
        idxs = jnp.where(slot == t, pos, idxs)
        s = jnp.where(ids == pos, -jnp.inf, s)
    return vals, idxs


_PAIRS = [(a, b) for a in range(PEER_TOPK) for b in range(PEER_TOPK) if (a + 1) * (b + 1) <= PEER_TOPK]


def _rows_from(src, picks, fill):
    if all(p is not None for p in picks) and picks[0] % SUBLANES == 0 and picks == list(range(picks[0], picks[0] + SUBLANES)):
        return src[picks[0]:picks[0] + SUBLANES, :]
    row = lax.broadcasted_iota(jnp.int32, (SUBLANES, src.shape[1]), 0)
    out = jnp.full((SUBLANES, src.shape[1]), fill, F32)
    for r, p in enumerate(picks):
        if p is not None:
            out = jnp.where(row == r, src[p:p + 1, :], out)
    return out


def _pair_candidates(sv1, sv2):
    pairs = _PAIRS + [None] * (-len(_PAIRS) % SUBLANES)
    width = sv1.shape[1]
    row = lax.broadcasted_iota(jnp.int32, (SUBLANES, width), 0)
    sums, ids = [], []
    for g0 in range(0, len(pairs), SUBLANES):
        grp = pairs[g0:g0 + SUBLANES]
        a_rows = _rows_from(sv1, [None if pr is None else pr[0] for pr in grp], -jnp.inf)
        b_rows = _rows_from(sv2, [None if pr is None else pr[1] for pr in grp], 0.0)
        sums.append(a_rows + b_rows)
        idv = jnp.full((SUBLANES, width), float(PEER_TOPK * PEER_TOPK), F32)
        for r, pr in enumerate(grp):
            if pr is not None:
                idv = jnp.where(row == r, float(pr[0] * PEER_TOPK + pr[1]), idv)
        ids.append(idv)
    return jnp.concatenate(sums, axis=0), jnp.concatenate(ids, axis=0)


def _router_body(qp_ref, keys_ref, idx_ref, g_ref):
    slot = lax.broadcasted_iota(jnp.int32, (PEER_TOPK, qp_ref.shape[0]), 0)
    for h in range(PEER_HEADS):
        sv, si = [], []
        for p in range(2):
            c0 = (h * 2 + p) * D_HALF
            q = qp_ref[:, c0:c0 + D_HALF].astype(BF16)
            s = lax.dot_general(keys_ref[h, p], q, (((1,), (1,)), ((), ())), preferred_element_type=F32)
            v, ix = _topk_rows(s, PEER_TOPK)
            sv.append(v)
            si.append(ix)
        cand, cand_id = _pair_candidates(sv[0], sv[1])
        top_s, pos = _topk_rows(cand, PEER_TOPK, cand_id)
        pos = pos.astype(jnp.int32)
        pa = pos >> 4
        pb = pos & (PEER_TOPK - 1)
        idx = jnp.zeros(slot.shape, F32)
        for t in range(PEER_TOPK):
            i1 = jnp.sum(jnp.where(slot == pa[t:t + 1, :], si[0], 0.0), axis=0, keepdims=True)
            i2 = jnp.sum(jnp.where(slot == pb[t:t + 1, :], si[1], 0.0), axis=0, keepdims=True)
            idx = jnp.where(slot == t, (i1 * N_KEYS + i2) * ROW_SUB, idx)
        idx = idx.astype(jnp.int32)
        e = jnp.exp(top_s - jnp.max(top_s, axis=0, keepdims=True))
        g = e / jnp.sum(e, axis=0, keepdims=True)
        idx_ref[h * PEER_TOPK:(h + 1) * PEER_TOPK, :] = idx
        g_ref[h * PEER_TOPK:(h + 1) * PEER_TOPK, :] = g


def _router(qp, keys):
    t = qp.shape[0]
    c = 256
    qw = PEER_HEADS * D_KEY
    return pl.pallas_call(
        _router_body,
        grid=(t // c,),
        in_specs=[pl.BlockSpec((c, qw), lambda i: (i, 0)),
                  pl.BlockSpec((PEER_HEADS, 2, N_KEYS, D_HALF), lambda i: (0, 0, 0, 0))],
        out_specs=[pl.BlockSpec((N_SEL, c), lambda i: (0, i)), pl.BlockSpec((N_SEL, c), lambda i: (0, i))],
        out_shape=[jax.ShapeDtypeStruct((N_SEL, t), jnp.int32), jax.ShapeDtypeStruct((N_SEL, t), F32)],
        compiler_params=pltpu.CompilerParams(dimension_semantics=("parallel",)),
        name="peer_router",
    )(qp, keys)


def _pack_body(t_ref, o_ref):
    o_ref[...] = pltpu.bitcast(t_ref[...].astype(BF16), jnp.int32)


def _pack_table(tab2):
    n8 = tab2.shape[0]
    r = 4096
    return pl.pallas_call(
        _pack_body,
        grid=(n8 // r,),
        in_specs=[pl.BlockSpec((r, LANES), lambda i: (i, 0))],
        out_specs=pl.BlockSpec((r // 2, LANES), lambda i: (i, 0)),
        out_shape=jax.ShapeDtypeStruct((n8 // 2, LANES), jnp.int32),
        compiler_params=pltpu.CompilerParams(dimension_semantics=("parallel",)),
        name="peer_pack",
    )(tab2)


PEER_UNROLL = SUBLANES
PEER_BLOCK = 128


def _load_table(tab_hbm, tab_s, sem):
    @pl.when(pl.program_id(0) == 0)
    def _():
        cp = pltpu.make_async_copy(tab_hbm, tab_s, sem)
        cp.start()
        cp.wait()


def _gather_rows(tab_s, idx_ref, t, m_ref):
    for j in range(N_SEL):
        r = pl.multiple_of(idx_ref[t, j], ROW_SUB)
        m_ref[j * ROW_SUB:(j + 1) * ROW_SUB, :] = tab_s[pl.ds(r, ROW_SUB), :]


def _split3_bf16(x):
    hi = x.astype(BF16).astype(F32)
    r1 = x - hi
    mid = r1.astype(BF16).astype(F32)
    lo = r1 - mid
    return jnp.concatenate([hi, mid, lo, jnp.zeros_like(x)], axis=0).astype(BF16)


def _sum3(y):
    return y[0:SUBLANES] + y[SUBLANES:2 * SUBLANES] + y[2 * SUBLANES:3 * SUBLANES]


def _diag_mask():
    p = lax.broadcasted_iota(jnp.int32, (SUBLANES, N_SEL * SUBLANES), 0)
    c = lax.broadcasted_iota(jnp.int32, (SUBLANES, N_SEL * SUBLANES), 1)
    return (c & (SUBLANES - 1)) == p


def _pipelined_groups(ngroups, gather, compute):
    assert ngroups >= 2 and ngroups % 2 == 0
    gather(0, 0)

    def pair(p, carry):
        g = 2 * p
        gather(g + 1, 1)
        compute(g, 0)
        gather(g + 2, 0)
        compute(g + 1, 1)
        return carry

    lax.fori_loop(0, ngroups // 2 - 1, pair, 0)
    gather(ngroups - 1, 1)
    compute(ngroups - 2, 0)
    compute(ngroups - 1, 1)


def _peer_u_body(idx_ref, x_ref, g_ref, tab_hbm, w_ref, tab_s, m_s, fold_s, p_s, s_s, sem, *, tb):
    _load_table(tab_hbm, tab_s, sem)

    @pl.when(pl.program_id(0) == 0)
    def _():
        r = lax.broadcasted_iota(jnp.int32, fold_s.shape, 0)
        c = lax.broadcasted_iota(jnp.int32, fold_s.shape, 1)
        fold_s[...] = jnp.where((r >> 3) == c, 1.0, 0.0).astype(BF16)

    diag = _diag_mask()

    def gather(gi, half):
        for u in range(PEER_UNROLL):
            _gather_rows(tab_s, idx_ref, gi * PEER_UNROLL + u, m_s.at[half * PEER_UNROLL + u])

    def compute(gi, half):
        for u in range(PEER_UNROLL):
            t = gi * PEER_UNROLL + u
            rows = pltpu.bitcast(m_s[half * PEER_UNROLL + u], BF16)
            p = _sum3(lax.dot_general(_split3_bf16(x_ref[t]), rows, NT_DIMS, preferred_element_type=F32))
            p_s[t] = jnp.where(diag, p, 0.0)

    _pipelined_groups(tb // PEER_UNROLL, gather, compute)

    pm = p_s[...].reshape(tb * SUBLANES, N_SEL * SUBLANES)
    hi = pm.astype(BF16)
    r1 = pm - hi.astype(F32)
    mid = r1.astype(BF16)
    lo = (r1 - mid.astype(F32)).astype(BF16)
    fold = fold_s[...]
    y = (jnp.dot(hi, fold, preferred_element_type=F32) + jnp.dot(mid, fold, preferred_element_type=F32)
         + jnp.dot(lo, fold, preferred_element_type=F32))
    s_s[...] = y.reshape(tb, SUBLANES, N_SEL)
    s = s_s[:, 0, :]
    for q in range(1, SUBLANES):
        s = s + s_s[:, q, :]
    w_ref[...] = g_ref[...] * jax.nn.gelu(s)


def _peer_u(idx_t, x3, g_t, tab):
    t = idx_t.shape[0]
    tb = PEER_BLOCK
    body = functools.partial(_peer_u_body, tb=tb)
    return pl.pallas_call(
        body,
        grid=(t // tb,),
        in_specs=[pl.BlockSpec((tb, N_SEL), lambda i: (i, 0), memory_space=pltpu.SMEM),
                  pl.BlockSpec((tb, SUBLANES, LANES), lambda i: (i, 0, 0)),
                  pl.BlockSpec((tb, N_SEL), lambda i: (i, 0)),
                  pl.BlockSpec(memory_space=pl.ANY)],
        out_specs=pl.BlockSpec((tb, N_SEL), lambda i: (i, 0)),
        out_shape=jax.ShapeDtypeStruct((t, N_SEL), F32),
        scratch_shapes=[pltpu.VMEM((N_EXPERTS * ROW_SUB, LANES), jnp.int32),
                        pltpu.VMEM((2 * PEER_UNROLL, N_SEL * ROW_SUB, LANES), jnp.int32),
                        pltpu.VMEM((N_SEL * SUBLANES, N_SEL), BF16),
                        pltpu.VMEM((tb, SUBLANES, N_SEL * SUBLANES), F32),
                        pltpu.VMEM((tb, SUBLANES, LANES), F32),
                        pltpu.SemaphoreType.DMA(())],
        compiler_params=pltpu.CompilerParams(dimension_semantics=("arbitrary",),
                                             vmem_limit_bytes=VMEM_LIMIT),
        name="peer_u",
    )(idx_t, x3, g_t, tab)


def _peer_v_body(idx_ref, w_ref, tab_hbm, o_ref, tab_s, m_s, spread_s, sem, *, tb):
    _load_table(tab_hbm, tab_s, sem)

    @pl.when(pl.program_id(0) == 0)
    def _():
        r = lax.broadcasted_iota(jnp.int32, spread_s.shape, 0)
        c = lax.broadcasted_iota(jnp.int32, spread_s.shape, 1)
        spread_s[...] = jnp.where((c >> 3) == r, 1.0, 0.0).astype(BF16)

    diag = _diag_mask()
    zero = jnp.zeros((SUBLANES, N_SEL * SUBLANES), F32)

    def gather(gi, half):
        for u in range(PEER_UNROLL):
            _gather_rows(tab_s, idx_ref, gi * PEER_UNROLL + u, m_s.at[half * PEER_UNROLL + u])

    def compute(gi, half):
        base = pl.multiple_of(gi * PEER_UNROLL, PEER_UNROLL)
        e = jnp.dot(_split3_bf16(w_ref[pl.ds(base, PEER_UNROLL), :]), spread_s[...], preferred_element_type=F32)
        for u in range(PEER_UNROLL):
            rows = pltpu.bitcast(m_s[half * PEER_UNROLL + u], BF16)
            terms = [jnp.where(diag, jnp.broadcast_to(e[SUBLANES * k + u:SUBLANES * k + u + 1, :], diag.shape), 0.0)
                     for k in range(3)]
            lhs = jnp.concatenate(terms + [zero], axis=0).astype(BF16)
            o_ref[base + u] = _sum3(jnp.dot(lhs, rows, preferred_element_type=F32))

    _pipelined_groups(tb // PEER_UNROLL, gather, compute)


def _peer_v(idx_t, w, tab):
    t = idx_t.shape[0]
    tb = PEER_BLOCK
    body = functools.partial(_peer_v_body, tb=tb)
    return pl.pallas_call(
        body,
        grid=(t // tb,),
        in_specs=[pl.BlockSpec((tb, N_SEL), lambda i: (i, 0), memory_space=pltpu.SMEM),
                  pl.BlockSpec((tb, N_SEL), lambda i: (i, 0)),
                  pl.BlockSpec(memory_space=pl.ANY)],
        out_specs=pl.BlockSpec((tb, SUBLANES, LANES), lambda i: (i, 0, 0)),
        out_shape=jax.ShapeDtypeStruct((t, SUBLANES, LANES), F32),
        scratch_shapes=[pltpu.VMEM((N_EXPERTS * ROW_SUB, LANES), jnp.int32),
                        pltpu.VMEM((2 * PEER_UNROLL, N_SEL * ROW_SUB, LANES), jnp.int32),
                        pltpu.VMEM((N_SEL, N_SEL * SUBLANES), BF16),
                        pltpu.SemaphoreType.DMA(())],
        compiler_params=pltpu.CompilerParams(dimension_semantics=("arbitrary",),
                                             vmem_limit_bytes=VMEM_LIMIT),
        name="peer_v",
    )(idx_t, w, tab)


def _final_body(x1_ref, o_ref, g_ref, y_ref):
    y_ref[...] = _rms(x1_ref[...] + o_ref[...], g_ref[...])


def _final(x1, o, g):
    t = x1.shape[0]
    tm = 512
    row = pl.BlockSpec((tm, D_MODEL), lambda i: (i, 0))
    return pl.pallas_call(
        _final_body,
        grid=(t // tm,),
        in_specs=[row, row, pl.BlockSpec((1, D_MODEL), lambda i: (0, 0))],
        out_specs=row,
        out_shape=jax.ShapeDtypeStruct((t, D_MODEL), F32),
        compiler_params=pltpu.CompilerParams(dimension_semantics=("parallel",)),
        name="final_norm",
    )(x1, o, g)


def _rel_bucket(rel):
    nb = NUM_BUCKETS // 2
    ret = jnp.where(rel > 0, nb, 0).astype(jnp.int32)
    n = jnp.abs(rel)
    max_exact = nb // 2
    nf = jnp.maximum(n, 1).astype(F32)
    large = max_exact + (jnp.log(nf / max_exact) / math.log(MAX_DISTANCE / max_exact) * (nb - max_exact)).astype(jnp.int32)
    large = jnp.minimum(large, nb - 1)
    return ret + jnp.where(n < max_exact, n, large)


def _bias_tables(rel_bias, tq):
    assert tq >= MAX_DISTANCE
    m = jnp.arange(2 * tq - 1, dtype=jnp.int32)
    rel = jnp.stack([d * tq + m - (tq - 1) for d in range(-BIAS_REACH, BIAS_REACH + 1)])
    w = jnp.transpose(rel_bias[_rel_bucket(rel)], (2, 0, 1)).astype(F32) * LOG2E
    period = 2 * tq
    v = jnp.concatenate([w[..., tq - 1::-1], jnp.zeros(w.shape[:2] + (1,), F32), w[..., :tq - 1:-1]], axis=-1)
    rep = jnp.tile(v, (1, 1, tq))[..., :tq * (period - 1)]
    return rep.reshape(w.shape[:2] + (tq, period - 1))[..., :tq]


def kernel(x_prompt, x_sample, norm1_g, w_in, b_gate, conv_w, conv_b, lru_wa_f, lru_ba_f, lru_wx_f, lru_bx_f, lru_lam_f, lru_wa_b, lru_ba_b, lru_wx_b, lru_bx_b, lru_lam_b, lam_q1, lam_k1, lam_q2, lam_k2, subln_g, rel_bias, w_rnn_out, w_attn_out, w_out, norm2_g, peer_wq, peer_keys, peer_u, peer_v, final_g):
    groups = [x_prompt.shape[:2], x_sample.shape[:2]]
    x2d = jnp.concatenate([x_prompt.reshape(-1, D_MODEL), x_sample.reshape(-1, D_MODEL)], axis=0)
    t = x2d.shape[0]
    l = 0
    row = lambda a: a.reshape(1, -1)

    w_main = jnp.concatenate([w_in[l][:, :QKV_COL0], w_in[l][:, QKV_COL1:]], axis=1).astype(BF16)
    proj = _inproj(x2d, row(norm1_g[l]), w_main)
    qk, vt = _qkv(x2d, row(norm1_g[l]), w_in[l][:, QKV_COL0:QKV_COL1].astype(BF16))

    btiles = _bias_tables(rel_bias, 512)
    lamv = jnp.stack([lam_q1[l], lam_k1[l], lam_q2[l], lam_k2[l]])
    lru_f = (conv_w[l], row(conv_b[l]), lru_wa_f[l].astype(BF16), row(lru_ba_f[l]),
             lru_wx_f[l].astype(BF16), row(lru_bx_f[l]), row(lru_lam_f[l]))
    lru_b = (conv_w[l], row(conv_b[l]), lru_wa_b[l].astype(BF16), row(lru_ba_b[l]),
             lru_wx_b[l].astype(BF16), row(lru_bx_b[l]), row(lru_lam_b[l]))
    hf, hb, attn = [], [], []
    row0 = 0
    for nb, s in groups:
        hf.append(_lru(proj, row0, nb, s, *lru_f, reverse=False))
        hb.append(_lru(proj, row0, nb, s, *lru_b, reverse=True))
        attn.append(_attention(qk, vt, row0, nb, s, btiles, lamv, row(subln_g[l])))
        row0 += nb * s
    hf, hb, attn = (jnp.concatenate(a, axis=0) for a in (hf, hb, attn))

    x1, xn2, qp = _merge(x2d, hf, hb, proj, attn, b_gate[l].reshape(2, D_MODEL),
                         w_rnn_out[l].astype(BF16), w_attn_out[l].astype(BF16), w_out[l].astype(BF16),
                         row(norm2_g[l]), peer_wq[l].astype(BF16))

    idx, g = _router(qp, peer_keys[l].astype(BF16))
    idx_t = idx.T
    g_t = g.T
    tab_u = _pack_table(peer_u[l].reshape(N_EXPERTS * SUBLANES, LANES))
    tab_v = _pack_table(peer_v[l].reshape(N_EXPERTS * SUBLANES, LANES))
    w = _peer_u(idx_t, xn2.reshape(t, SUBLANES, LANES), g_t, tab_u)
    o = _peer_v(idx_t, w, tab_v).reshape(t, D_MODEL)

    y = _final(x1, o, row(final_g))
    n0 = groups[0][0] * groups[0][1]
    return (y[:n0].reshape(x_prompt.shape), y[n0:].reshape(x_sample.shape))
```

```python
import functools
import math

import jax
import jax.numpy as jnp
from jax import lax
from jax.experimental import pallas as pl
from jax.experimental.pallas import tpu as pltpu

F32 = jnp.float32
BF16 = jnp.bfloat16

D_MODEL = 1024
D_RNN = 1024
LRU_BLOCKS = 4
LRU_BW = D_RNN // LRU_BLOCKS
LRU_C = 8.0
CONV_W = 4
N_HEADS = 8
HEAD_DIM = 64
V_DIM = 2 * HEAD_DIM
V_AUG = V_DIM + 16
ATTN_W = N_HEADS * V_DIM
NUM_BUCKETS = 32
MAX_DISTANCE = 128
PEER_HEADS = 8
N_KEYS = 128
N_EXPERTS = N_KEYS * N_KEYS
PEER_TOPK = 16
D_KEY = 256
D_HALF = D_KEY // 2
N_SEL = PEER_HEADS * PEER_TOPK
IN_W = 2 * D_RNN + 3 * ATTN_W + 2 * D_MODEL
EPS = 1e-6
LAM_INIT = 0.8 - 0.6 * math.exp(-0.3 * 0)
LOG2E = math.log2(math.e)
NT_DIMS = (((1,), (1,)), ((), ()))
BIAS_REACH = 2

COL_XRNN, COL_GRNN, COL_GLR, COL_GLA = range(4)
QKV_COL0 = 2 * D_RNN
QKV_COL1 = QKV_COL0 + 3 * ATTN_W

SUBLANES = 8
LANES = 128
ROW_WORDS = D_MODEL // 2
ROW_SUB = ROW_WORDS // LANES
VMEM_LIMIT = 56 * 1024 * 1024


def _rms(x, g):
    return x * lax.rsqrt(jnp.mean(x * x, axis=-1, keepdims=True) + EPS) * g


def _inproj_body(x_ref, g_ref, w_ref, o_ref, xn_ref):
    @pl.when(pl.program_id(1) == 0)
    def _():
        xn_ref[...] = _rms(x_ref[...], g_ref[...]).astype(BF16)

    o_ref[...] = jnp.dot(xn_ref[...], w_ref[...], preferred_element_type=F32)


def _inproj(x2d, g, w):
    t = x2d.shape[0]
    n_out = w.shape[1]
    tm, tn = 1024, 1024
    return pl.pallas_call(
        _inproj_body,
        grid=(t // tm, n_out // tn),
        in_specs=[
            pl.BlockSpec((tm, D_MODEL), lambda i, j: (i, 0)),
            pl.BlockSpec((1, D_MODEL), lambda i, j: (0, 0)),
            pl.BlockSpec((D_MODEL, tn), lambda i, j: (0, j)),
        ],
        out_specs=pl.BlockSpec((tm, tn), lambda i, j: (i, j)),
        out_shape=jax.ShapeDtypeStruct((t, n_out), F32),
        scratch_shapes=[pltpu.VMEM((tm, D_MODEL), BF16)],
        compiler_params=pltpu.CompilerParams(dimension_semantics=("parallel", "arbitrary")),
        name="inproj",
    )(x2d, g, w)


def _qkv_body(x_ref, g_ref, w_ref, qk_ref, vt_ref, xn_ref):
    j = pl.program_id(1)

    @pl.when(j == 0)
    def _():
        xn_ref[...] = _rms(x_ref[...], g_ref[...]).astype(BF16)

    res = jnp.dot(xn_ref[...], w_ref[...], preferred_element_type=F32)

    @pl.when(j < 2)
    def _():
        for h in range(N_HEADS):
            qk_ref[h] = res[:, h * V_DIM:(h + 1) * V_DIM].astype(BF16)

    @pl.when(j == 2)
    def _():
        for h in range(N_HEADS):
            vt_ref[h, 0:V_DIM, :] = res[:, h * V_DIM:(h + 1) * V_DIM].T.astype(BF16)
            vt_ref[h, V_DIM:V_AUG, :] = jnp.ones((V_AUG - V_DIM, res.shape[0]), BF16)


def _qkv(x2d, g, w):
    t = x2d.shape[0]
    tm = 1024
    return pl.pallas_call(
        _qkv_body,
        grid=(t // tm, 3),
        in_specs=[
            pl.BlockSpec((tm, D_MODEL), lambda i, j: (i, 0)),
            pl.BlockSpec((1, D_MODEL), lambda i, j: (0, 0)),
            pl.BlockSpec((D_MODEL, ATTN_W), lambda i, j: (0, j)),
        ],
        out_specs=[pl.BlockSpec((None, N_HEADS, tm, V_DIM), lambda i, j: (jnp.minimum(j, 1), 0, i, 0)),
                   pl.BlockSpec((N_HEADS, V_AUG, tm), lambda i, j: (0, 0, i))],
        out_shape=[jax.ShapeDtypeStruct((2, N_HEADS, t, V_DIM), BF16),
                   jax.ShapeDtypeStruct((N_HEADS, V_AUG, t), BF16)],
        scratch_shapes=[pltpu.VMEM((tm, D_MODEL), BF16)],
        compiler_params=pltpu.CompilerParams(dimension_semantics=("parallel", "arbitrary")),
        name="qkv_proj",
    )(x2d, g, w)


def _lru_body(xc_ref, xp_ref, xn_ref, cw_ref, cb_ref, wa_ref, ba_ref, wx_ref, bx_ref, lam_ref,
              h_ref, a_s, b_s, carry, *, reverse, nchunks, tc):
    i = pl.program_id(1)
    c = (nchunks - 1 - i) if reverse else i
    x = xc_ref[...]
    rows = lax.broadcasted_iota(jnp.int32, (tc, D_RNN), 0)
    prev = jnp.where(c > 0, xp_ref[SUBLANES - 1:SUBLANES, :], 0.0)
    nxt0 = jnp.where(c < nchunks - 1, xn_ref[0:1, :], 0.0)
    nxt1 = jnp.where(c < nchunks - 1, xn_ref[1:2, :], 0.0)
    xm1 = jnp.where(rows == 0, prev, pltpu.roll(x, 1, 0))
    xp1 = jnp.where(rows == tc - 1, nxt0, pltpu.roll(x, tc - 1, 0))
    xp2 = jnp.where(rows == tc - 2, nxt0, jnp.where(rows == tc - 1, nxt1, pltpu.roll(x, tc - 2, 0)))
    cw = cw_ref[...]
    xc = cb_ref[...] + xm1 * cw[0:1] + x * cw[1:2] + xp1 * cw[2:3] + xp2 * cw[3:4]

    xcb = xc.astype(BF16)
    r_parts, i_parts = [], []
    for n in range(LRU_BLOCKS):
        xs = xcb[:, n * LRU_BW:(n + 1) * LRU_BW]
        r_parts.append(jnp.dot(xs, wa_ref[n], preferred_element_type=F32))
        i_parts.append(jnp.dot(xs, wx_ref[n], preferred_element_type=F32))
    r = jax.nn.sigmoid(jnp.concatenate(r_parts, axis=1) + ba_ref[...])
    ig = jax.nn.sigmoid(jnp.concatenate(i_parts, axis=1) + bx_ref[...])
    z = -lam_ref[...]
    softplus = jnp.maximum(z, 0.0) + jnp.log1p(jnp.exp(-jnp.abs(z)))
    log_a = -LRU_C * r * softplus
    a = jnp.exp(log_a)
    mult = jnp.sqrt(-jnp.tanh(log_a) * (a * a + 1.0))
    edge_row = tc - 1 if reverse else 0
    edge_chunk = nchunks - 1 if reverse else 0
    mult = jnp.where(rows == jnp.where(c == edge_chunk, edge_row, -1), 1.0, mult)
    a_s[...] = a
    b_s[...] = mult * ig * xc

    row8 = lax.broadcasted_iota(jnp.int32, (SUBLANES, D_RNN), 0)
    ngroups = tc // SUBLANES
    h0 = jnp.where(i == 0, 0.0, carry[...])

    def step(gi, h):
        g = (ngroups - 1 - gi) if reverse else gi
        off = pl.multiple_of(g * SUBLANES, SUBLANES)
        av = a_s[pl.ds(off, SUBLANES), :]
        bv = b_s[pl.ds(off, SUBLANES), :]
        for s in (1, 2, 4):
            if reverse:
                keep = row8 < SUBLANES - s
                shift = SUBLANES - s
            else:
                keep = row8 >= s
                shift = s
            a_sh = jnp.where(keep, pltpu.roll(av, shift, 0), 1.0)
            b_sh = jnp.where(keep, pltpu.roll(bv, shift, 0), 0.0)
            bv = av * b_sh + bv
            av = av * a_sh
        hv = av * h + bv
        h_ref[pl.ds(off, SUBLANES), :] = hv
        last = hv[0:1, :] if reverse else hv[SUBLANES - 1:SUBLANES, :]
        return jnp.broadcast_to(last, (SUBLANES, D_RNN))

    carry[...] = lax.fori_loop(0, ngroups, step, h0)


def _lru(proj, row0, nb, s, cw, cb, wa, ba, wx, bx, lam, reverse):
    tc = 512
    nchunks = s // tc
    t_all = proj.shape[0]
    blk0 = row0 // tc
    last8 = t_all // SUBLANES - 1

    def cidx(i):
        return (nchunks - 1 - i) if reverse else i

    def cur(b, i):
        return (blk0 + b * nchunks + cidx(i), COL_XRNN)

    def prev8(b, i):
        r = (blk0 + b * nchunks + cidx(i)) * (tc // SUBLANES) - 1
        return (jnp.maximum(r, 0), COL_XRNN)

    def next8(b, i):
        r = (blk0 + b * nchunks + cidx(i) + 1) * (tc // SUBLANES)
        return (jnp.minimum(r, last8), COL_XRNN)

    full = lambda shape: pl.BlockSpec(shape, lambda b, i: (0,) * len(shape))
    body = functools.partial(_lru_body, reverse=reverse, nchunks=nchunks, tc=tc)
    return pl.pallas_call(
        body,
        grid=(nb, nchunks),
        in_specs=[
            pl.BlockSpec((tc, D_RNN), cur),
            pl.BlockSpec((SUBLANES, D_RNN), prev8),
            pl.BlockSpec((SUBLANES, D_RNN), next8),
            full((CONV_W, D_RNN)), full((1, D_RNN)),
            full((LRU_BLOCKS, LRU_BW, LRU_BW)), full((1, D_RNN)),
            full((LRU_BLOCKS, LRU_BW, LRU_BW)), full((1, D_RNN)),
            full((1, D_RNN)),
        ],
        out_specs=pl.BlockSpec((tc, D_RNN), lambda b, i: (b * nchunks + cidx(i), 0)),
        out_shape=jax.ShapeDtypeStruct((nb * s, D_RNN), F32),
        scratch_shapes=[pltpu.VMEM((tc, D_RNN), F32), pltpu.VMEM((tc, D_RNN), F32),
                        pltpu.VMEM((SUBLANES, D_RNN), F32)],
        compiler_params=pltpu.CompilerParams(dimension_semantics=("arbitrary", "arbitrary")),
        name="lru_bwd" if reverse else "lru_fwd",
    )(proj, proj, proj, cw, cb, wa, ba, wx, bx, lam)


def _attn_body(q_ref, k_ref, vt_ref, bt_ref, lamv_ref, g_ref, o_ref,
               q1_s, q2_s, m1_s, acc1_s, m2_s, acc2_s, sa_s, sb_s, *, nkv, tkv):
    i = pl.program_id(2)

    q = q_ref[...].astype(F32) * (HEAD_DIM ** -0.5 * LOG2E)
    lane = lax.broadcasted_iota(jnp.int32, q.shape, 1)
    q1_s[...] = jnp.where(lane < HEAD_DIM, q, 0.0).astype(BF16)
    q2_s[...] = jnp.where(lane >= HEAD_DIM, q, 0.0).astype(BF16)
    for m_s, acc_s in ((m1_s, acc1_s), (m2_s, acc2_s)):
        m_s[...] = jnp.full(m_s.shape, -jnp.inf, F32)
        acc_s[...] = jnp.zeros(acc_s.shape, F32)

    def scores(j, buf):
        off = pl.multiple_of(j * tkv, tkv)
        k = k_ref[pl.ds(off, tkv), :]
        bias = bt_ref[jnp.clip(j - i, -BIAS_REACH, BIAS_REACH) + BIAS_REACH]
        for c, q_s in enumerate((q1_s, q2_s)):
            buf[c] = lax.dot_general(k, q_s[...], NT_DIMS, preferred_element_type=F32) + bias

    def accumulate(j, buf):
        off = pl.multiple_of(j * tkv, tkv)
        vt = vt_ref[:, pl.ds(off, tkv)]
        for c, (m_s, acc_s) in enumerate(((m1_s, acc1_s), (m2_s, acc2_s))):
            s = buf[c]
            m_old = m_s[...]
            m_new = jnp.maximum(m_old, jnp.max(s, axis=0, keepdims=True))
            alpha = jnp.exp2(m_old - m_new)
            p = jnp.exp2((s - m_new).astype(BF16))
            acc_s[...] = alpha * acc_s[...] + jnp.dot(vt, p, preferred_element_type=F32)
            m_s[...] = m_new

    scores(0, sa_s)

    def pair(u, carry):
        j = 2 * u
        scores(j + 1, sb_s)
        accumulate(j, sa_s)
        scores(j + 2, sa_s)
        accumulate(j + 1, sb_s)
        return carry

    lax.fori_loop(0, nkv // 2 - 1, pair, 0)
    scores(nkv - 1, sb_s)
    accumulate(nkv - 2, sa_s)
    accumulate(nkv - 1, sb_s)

    lv = lamv_ref[...]
    lam = (jnp.exp(jnp.sum(lv[0:1] * lv[1:2], axis=-1, keepdims=True))
           - jnp.exp(jnp.sum(lv[2:3] * lv[3:4], axis=-1, keepdims=True)) + LAM_INIT)
    o = (acc1_s[0:V_DIM, :] / acc1_s[V_DIM:V_DIM + 1, :]
         - lam * (acc2_s[0:V_DIM, :] / acc2_s[V_DIM:V_DIM + 1, :]))
    o_ref[...] = _rms(o.T, g_ref[...]) * (1.0 - LAM_INIT)


def _attention(qk, vt, row0, nb, s, btiles, lamv, subln_g):
    tq = btiles.shape[-1]
    nq = s // tq
    assert nq >= 2 and nq % 2 == 0
    blk0 = row0 // tq
    seq0 = row0 // s

    body = functools.partial(_attn_body, nkv=nq, tkv=tq)
    stat = pltpu.VMEM((1, tq), F32)
    acc = pltpu.VMEM((V_AUG, tq), F32)
    sbuf = pltpu.VMEM((2, tq, tq), F32)
    return pl.pallas_call(
        body,
        grid=(nb, N_HEADS, nq),
        in_specs=[
            pl.BlockSpec((None, None, tq, V_DIM), lambda b, h, i: (0, h, blk0 + b * nq + i, 0)),
            pl.BlockSpec((None, None, s, V_DIM), lambda b, h, i: (1, h, seq0 + b, 0)),
            pl.BlockSpec((None, V_AUG, s), lambda b, h, i: (h, 0, seq0 + b)),
            pl.BlockSpec((None, 2 * BIAS_REACH + 1, tq, tq), lambda b, h, i: (h, 0, 0, 0)),
            pl.BlockSpec((4, HEAD_DIM), lambda b, h, i: (0, 0)),
            pl.BlockSpec((1, V_DIM), lambda b, h, i: (0, 0)),
        ],
        out_specs=pl.BlockSpec((tq, V_DIM), lambda b, h, i: (b * nq + i, h)),
        out_shape=jax.ShapeDtypeStruct((nb * s, ATTN_W), F32),
        scratch_shapes=[pltpu.VMEM((tq, V_DIM), BF16), pltpu.VMEM((tq, V_DIM), BF16),
                        stat, acc, stat, acc, sbuf, sbuf],
        compiler_params=pltpu.CompilerParams(
            dimension_semantics=("parallel", "parallel", "arbitrary"),
            vmem_limit_bytes=VMEM_LIMIT),
        name="diff_attn",
    )(qk, qk, vt, btiles, lamv, subln_g)


def _merge_body(x_ref, hf0_ref, hf1_ref, hb0_ref, hb1_ref, at0_ref, at1_ref, grnn_ref, glr_ref, gla_ref, bg_ref,
                wr_ref, wa_ref, wo_ref, n2_ref, wq_ref, x1_ref, xn2_ref, qp_ref, *, nblk0):
    first = pl.program_id(0) < nblk0
    hf = jnp.where(first, hf0_ref[...], hf1_ref[...])
    hb = jnp.where(first, hb0_ref[...], hb1_ref[...])
    attn = jnp.where(first, at0_ref[...], at1_ref[...])
    hg = ((hf + hb) * jax.nn.gelu(grnn_ref[...])).astype(BF16)
    y_rnn = jnp.dot(hg, wr_ref[...], preferred_element_type=F32)
    y_attn = jnp.dot(attn.astype(BF16), wa_ref[...], preferred_element_type=F32)
    bg = bg_ref[...]
    g_r = jax.nn.sigmoid(glr_ref[...] + bg[0:1])
    g_a = jax.nn.sigmoid(gla_ref[...] + bg[1:2])
    merged = (g_r * y_rnn + g_a * y_attn).astype(BF16)
    x1 = x_ref[...] + jnp.dot(merged, wo_ref[...], preferred_element_type=F32)
    x1_ref[...] = x1
    xn2 = _rms(x1, n2_ref[...])
    xn2_ref[...] = xn2
    qp_ref[...] = jnp.dot(xn2.astype(BF16), wq_ref[...], preferred_element_type=F32)


def _merge(x2d, hf, hb, proj, attn, bg, wr, wa, wo, n2, wq):
    t = x2d.shape[0]
    tm = 256
    nblk0 = hf[0].shape[0] // tm
    nblk1 = hf[1].shape[0] // tm
    row = lambda c: pl.BlockSpec((tm, D_MODEL), lambda i: (i, c))
    grp0 = pl.BlockSpec((tm, D_MODEL), lambda i: (jnp.minimum(i, nblk0 - 1), 0))
    grp1 = pl.BlockSpec((tm, D_MODEL), lambda i: (jnp.clip(i - nblk0, 0, nblk1 - 1), 0))
    full = lambda shape: pl.BlockSpec(shape, lambda i: (0,) * len(shape))
    qw = PEER_HEADS * D_KEY
    return pl.pallas_call(
        functools.partial(_merge_body, nblk0=nblk0),
        grid=(t // tm,),
        in_specs=[row(0), grp0, grp1, grp0, grp1, grp0, grp1, row(COL_GRNN), row(COL_GLR), row(COL_GLA),
                  full((2, D_MODEL)), full((D_RNN, D_MODEL)), full((ATTN_W, D_MODEL)),
                  full((D_MODEL, D_MODEL)), full((1, D_MODEL)), full((D_MODEL, qw))],
        out_specs=[row(0), row(0), pl.BlockSpec((tm, qw), lambda i: (i, 0))],
        out_shape=[jax.ShapeDtypeStruct((t, D_MODEL), F32), jax.ShapeDtypeStruct((t, D_MODEL), F32),
                   jax.ShapeDtypeStruct((t, qw), F32)],
        compiler_params=pltpu.CompilerParams(dimension_semantics=("parallel",),
                                             vmem_limit_bytes=VMEM_LIMIT),
        name="merge_proj",
    )(x2d, hf[0], hf[1], hb[0], hb[1], attn[0], attn[1], proj, proj, proj, bg, wr, wa, wo, n2, wq)


def _topk_rows(s, k, ids=None):
    if ids is None:
        ids = lax.broadcasted_iota(jnp.int32, s.shape, 0).astype(F32)
    slot = lax.broadcasted_iota(jnp.int32, (k, s.shape[1]), 0)
    vals = jnp.zeros((k, s.shape[1]), F32)
    idxs = jnp.zeros((k, s.shape[1]), F32)
    for t in range(k):
        m = jnp.max(s, axis=0, keepdims=True)
        pos = jnp.min(jnp.where(s == m, ids, jnp.inf), axis=0, keepdims=True)
        vals = jnp.where(slot == t, m, vals)
        idxs = jnp.where(slot == t, pos, idxs)
        s = jnp.where(ids == pos, -jnp.inf, s)
    return vals, idxs


_PAIRS = [(a, b) for a in range(PEER_TOPK) for b in range(PEER_TOPK) if (a + 1) * (b + 1) <= PEER_TOPK]


def _rows_from(src, picks, fill):
    if all(p is not None for p in picks) and picks[0] % SUBLANES == 0 and picks == list(range(picks[0], picks[0] + SUBLANES)):
        return src[picks[0]:picks[0] + SUBLANES, :]
    row = lax.broadcasted_iota(jnp.int32, (SUBLANES, src.shape[1]), 0)
    out = jnp.full((SUBLANES, src.shape[1]), fill, F32)
    for r, p in enumerate(picks):
        if p is not None:
            out = jnp.where(row == r, src[p:p + 1, :], out)
    return out


def _pair_candidates(sv1, sv2):
    pairs = _PAIRS + [None] * (-len(_PAIRS) % SUBLANES)
    width = sv1.shape[1]
    row = lax.broadcasted_iota(jnp.int32, (SUBLANES, width), 0)
    sums, ids = [], []
    for g0 in range(0, len(pairs), SUBLANES):
        grp = pairs[g0:g0 + SUBLANES]
        a_rows = _rows_from(sv1, [None if pr is None else pr[0] for pr in grp], -jnp.inf)
        b_rows = _rows_from(sv2, [None if pr is None else pr[1] for pr in grp], 0.0)
        sums.append(a_rows + b_rows)
        idv = jnp.full((SUBLANES, width), float(PEER_TOPK * PEER_TOPK), F32)
        for r, pr in enumerate(grp):
            if pr is not None:
                idv = jnp.where(row == r, float(pr[0] * PEER_TOPK + pr[1]), idv)
        ids.append(idv)
    return jnp.concatenate(sums, axis=0), jnp.concatenate(ids, axis=0)


def _router_body(qp_ref, keys_ref, idx_ref, g_ref):
    slot = lax.broadcasted_iota(jnp.int32, (PEER_TOPK, qp_ref.shape[0]), 0)
    for h in range(PEER_HEADS):
        sv, si = [], []
        for p in range(2):
            c0 = (h * 2 + p) * D_HALF
            q = qp_ref[:, c0:c0 + D_HALF].astype(BF16)
            s = lax.dot_general(keys_ref[h, p], q, (((1,), (1,)), ((), ())), preferred_element_type=F32)
            v, ix = _topk_rows(s, PEER_TOPK)
            sv.append(v)
            si.append(ix)
        cand, cand_id = _pair_candidates(sv[0], sv[1])
        top_s, pos = _topk_rows(cand, PEER_TOPK, cand_id)
        pos = pos.astype(jnp.int32)
        pa = pos >> 4
        pb = pos & (PEER_TOPK - 1)
        idx = jnp.zeros(slot.shape, F32)
        for t in range(PEER_TOPK):
            i1 = jnp.sum(jnp.where(slot == pa[t:t + 1, :], si[0], 0.0), axis=0, keepdims=True)
            i2 = jnp.sum(jnp.where(slot == pb[t:t + 1, :], si[1], 0.0), axis=0, keepdims=True)
            idx = jnp.where(slot == t, (i1 * N_KEYS + i2) * ROW_SUB, idx)
        idx = idx.astype(jnp.int32)
        e = jnp.exp(top_s - jnp.max(top_s, axis=0, keepdims=True))
        g = e / jnp.sum(e, axis=0, keepdims=True)
        idx_ref[h * PEER_TOPK:(h + 1) * PEER_TOPK, :] = idx
        g_ref[h * PEER_TOPK:(h + 1) * PEER_TOPK, :] = g


def _router(qp, keys):
    t = qp.shape[0]
    c = 256
    qw = PEER_HEADS * D_KEY
    return pl.pallas_call(
        _router_body,
        grid=(t // c,),
        in_specs=[pl.BlockSpec((c, qw), lambda i: (i, 0)),
                  pl.BlockSpec((PEER_HEADS, 2, N_KEYS, D_HALF), lambda i: (0, 0, 0, 0))],
        out_specs=[pl.BlockSpec((N_SEL, c), lambda i: (0, i)), pl.BlockSpec((N_SEL, c), lambda i: (0, i))],
        out_shape=[jax.ShapeDtypeStruct((N_SEL, t), jnp.int32), jax.ShapeDtypeStruct((N_SEL, t), F32)],
        compiler_params=pltpu.CompilerParams(dimension_semantics=("parallel",)),
        name="peer_router",
    )(qp, keys)


def _pack_body(t_ref, o_ref):
    o_ref[...] = pltpu.bitcast(t_ref[...].astype(BF16), jnp.int32)


def _pack_table(tab2):
    n8 = tab2.shape[0]
    r = 4096
    return pl.pallas_call(
        _pack_body,
        grid=(n8 // r,),
        in_specs=[pl.BlockSpec((r, LANES), lambda i: (i, 0))],
        out_specs=pl.BlockSpec((r // 2, LANES), lambda i: (i, 0)),
        out_shape=jax.ShapeDtypeStruct((n8 // 2, LANES), jnp.int32),
        compiler_params=pltpu.CompilerParams(dimension_semantics=("parallel",)),
        name="peer_pack",
    )(tab2)


PEER_UNROLL = SUBLANES
PEER_U_BLOCK = 128
PEER_V_BLOCK = 256


def _load_table(tab_hbm, tab_s, sem):
    @pl.when(pl.program_id(0) == 0)
    def _():
        cp = pltpu.make_async_copy(tab_hbm, tab_s, sem)
        cp.start()
        cp.wait()


def _gather_rows(tab_s, idx_ref, t, m_ref):
    for j in range(N_SEL):
        r = pl.multiple_of(idx_ref[t, j], ROW_SUB)
        m_ref[j * ROW_SUB:(j + 1) * ROW_SUB, :] = tab_s[pl.ds(r, ROW_SUB), :]


def _split3_bf16(x):
    hi = x.astype(BF16).astype(F32)
    r1 = x - hi
    mid = r1.astype(BF16).astype(F32)
    lo = r1 - mid
    return jnp.concatenate([hi, mid, lo, jnp.zeros_like(x)], axis=0).astype(BF16)


def _sum3(y):
    return y[0:SUBLANES] + y[SUBLANES:2 * SUBLANES] + y[2 * SUBLANES:3 * SUBLANES]


def _diag_mask():
    p = lax.broadcasted_iota(jnp.int32, (SUBLANES, N_SEL * SUBLANES), 0)
    c = lax.broadcasted_iota(jnp.int32, (SUBLANES, N_SEL * SUBLANES), 1)
    return (c & (SUBLANES - 1)) == p


def _pipelined_groups(ngroups, gather, compute):
    assert ngroups >= 2 and ngroups % 2 == 0
    gather(0, 0)

    def pair(p, carry):
        g = 2 * p
        gather(g + 1, 1)
        compute(g, 0)
        gather(g + 2, 0)
        compute(g + 1, 1)
        return carry

    lax.fori_loop(0, ngroups // 2 - 1, pair, 0)
    gather(ngroups - 1, 1)
    compute(ngroups - 2, 0)
    compute(ngroups - 1, 1)


def _peer_u_body(idx_ref, x_ref, g_ref, tab_hbm, w_ref, tab_s, m_s, fold_s, p_s, s_s, sem, *, tb):
    _load_table(tab_hbm, tab_s, sem)

    @pl.when(pl.program_id(0) == 0)
    def _():
        r = lax.broadcasted_iota(jnp.int32, fold_s.shape, 0)
        c = lax.broadcasted_iota(jnp.int32, fold_s.shape, 1)
        fold_s[...] = jnp.where((r >> 3) == c, 1.0, 0.0).astype(BF16)

    diag = _diag_mask()

    def gather(gi, half):
        for u in range(PEER_UNROLL):
            _gather_rows(tab_s, idx_ref, gi * PEER_UNROLL + u, m_s.at[half * PEER_UNROLL + u])

    def compute(gi, half):
        for u in range(PEER_UNROLL):
            t = gi * PEER_UNROLL + u
            rows = pltpu.bitcast(m_s[half * PEER_UNROLL + u], BF16)
            p = _sum3(lax.dot_general(_split3_bf16(x_ref[t]), rows, NT_DIMS, preferred_element_type=F32))
            p_s[t] = jnp.where(diag, p, 0.0)

    _pipelined_groups(tb // PEER_UNROLL, gather, compute)

    pm = p_s[...].reshape(tb * SUBLANES, N_SEL * SUBLANES)
    hi = pm.astype(BF16)
    r1 = pm - hi.astype(F32)
    mid = r1.astype(BF16)
    lo = (r1 - mid.astype(F32)).astype(BF16)
    fold = fold_s[...]
    y = (jnp.dot(hi, fold, preferred_element_type=F32) + jnp.dot(mid, fold, preferred_element_type=F32)
         + jnp.dot(lo, fold, preferred_element_type=F32))
    s_s[...] = y.reshape(tb, SUBLANES, N_SEL)
    s = s_s[:, 0, :]
    for q in range(1, SUBLANES):
        s = s + s_s[:, q, :]
    w_ref[...] = g_ref[...] * jax.nn.gelu(s)


def _peer_u(idx_t, x3, g_t, tab):
    t = idx_t.shape[0]
    tb = PEER_U_BLOCK
    body = functools.partial(_peer_u_body, tb=tb)
    return pl.pallas_call(
        body,
        grid=(t // tb,),
        in_specs=[pl.BlockSpec((tb, N_SEL), lambda i: (i, 0), memory_space=pltpu.SMEM),
                  pl.BlockSpec((tb, SUBLANES, LANES), lambda i: (i, 0, 0)),
                  pl.BlockSpec((tb, N_SEL), lambda i: (i, 0)),
                  pl.BlockSpec(memory_space=pl.ANY)],
        out_specs=pl.BlockSpec((tb, N_SEL), lambda i: (i, 0)),
        out_shape=jax.ShapeDtypeStruct((t, N_SEL), F32),
        scratch_shapes=[pltpu.VMEM((N_EXPERTS * ROW_SUB, LANES), jnp.int32),
                        pltpu.VMEM((2 * PEER_UNROLL, N_SEL * ROW_SUB, LANES), jnp.int32),
                        pltpu.VMEM((N_SEL * SUBLANES, N_SEL), BF16),
                        pltpu.VMEM((tb, SUBLANES, N_SEL * SUBLANES), F32),
                        pltpu.VMEM((tb, SUBLANES, LANES), F32),
                        pltpu.SemaphoreType.DMA(())],
        compiler_params=pltpu.CompilerParams(dimension_semantics=("arbitrary",),
                                             vmem_limit_bytes=VMEM_LIMIT),
        name="peer_u",
    )(idx_t, x3, g_t, tab)


def _peer_v_body(idx_ref, w_ref, tab_hbm, o_ref, tab_s, m_s, spread_s, sem, *, tb):
    _load_table(tab_hbm, tab_s, sem)

    @pl.when(pl.program_id(0) == 0)
    def _():
        r = lax.broadcasted_iota(jnp.int32, spread_s.shape, 0)
        c = lax.broadcasted_iota(jnp.int32, spread_s.shape, 1)
        spread_s[...] = jnp.where((c >> 3) == r, 1.0, 0.0).astype(BF16)

    diag = _diag_mask()
    zero = jnp.zeros((SUBLANES, N_SEL * SUBLANES), F32)

    def gather(gi, half):
        for u in range(PEER_UNROLL):
            _gather_rows(tab_s, idx_ref, gi * PEER_UNROLL + u, m_s.at[half * PEER_UNROLL + u])

    def compute(gi, half):
        base = pl.multiple_of(gi * PEER_UNROLL, PEER_UNROLL)
        e = jnp.dot(_split3_bf16(w_ref[pl.ds(base, PEER_UNROLL), :]), spread_s[...], preferred_element_type=F32)
        for u in range(PEER_UNROLL):
            rows = pltpu.bitcast(m_s[half * PEER_UNROLL + u], BF16)
            terms = [jnp.where(diag, jnp.broadcast_to(e[SUBLANES * k + u:SUBLANES * k + u + 1, :], diag.shape), 0.0)
                     for k in range(3)]
            lhs = jnp.concatenate(terms + [zero], axis=0).astype(BF16)
            o_ref[base + u] = _sum3(jnp.dot(lhs, rows, preferred_element_type=F32))

    _pipelined_groups(tb // PEER_UNROLL, gather, compute)


def _peer_v(idx_t, w, tab):
    t = idx_t.shape[0]
    tb = PEER_V_BLOCK
    body = functools.partial(_peer_v_body, tb=tb)
    return pl.pallas_call(
        body,
        grid=(t // tb,),
        in_specs=[pl.BlockSpec((tb, N_SEL), lambda i: (i, 0), memory_space=pltpu.SMEM),
                  pl.BlockSpec((tb, N_SEL), lambda i: (i, 0)),
                  pl.BlockSpec(memory_space=pl.ANY)],
        out_specs=pl.BlockSpec((tb, SUBLANES, LANES), lambda i: (i, 0, 0)),
        out_shape=jax.ShapeDtypeStruct((t, SUBLANES, LANES), F32),
        scratch_shapes=[pltpu.VMEM((N_EXPERTS * ROW_SUB, LANES), jnp.int32),
                        pltpu.VMEM((2 * PEER_UNROLL, N_SEL * ROW_SUB, LANES), jnp.int32),
                        pltpu.VMEM((N_SEL, N_SEL * SUBLANES), BF16),
                        pltpu.SemaphoreType.DMA(())],
        compiler_params=pltpu.CompilerParams(dimension_semantics=("arbitrary",),
                                             vmem_limit_bytes=VMEM_LIMIT),
        name="peer_v",
    )(idx_t, w, tab)


def _final_body(x1_ref, o_ref, g_ref, y_ref):
    y_ref[...] = _rms(x1_ref[...] + o_ref[...], g_ref[...])


def _final(x1, o, g):
    t = x1.shape[0]
    tm = 512
    row = pl.BlockSpec((tm, D_MODEL), lambda i: (i, 0))
    return pl.pallas_call(
        _final_body,
        grid=(t // tm,),
        in_specs=[row, row, pl.BlockSpec((1, D_MODEL), lambda i: (0, 0))],
        out_specs=row,
        out_shape=jax.ShapeDtypeStruct((t, D_MODEL), F32),
        compiler_params=pltpu.CompilerParams(dimension_semantics=("parallel",)),
        name="final_norm",
    )(x1, o, g)


def _rel_bucket(rel):
    nb = NUM_BUCKETS // 2
    ret = jnp.where(rel > 0, nb, 0).astype(jnp.int32)
    n = jnp.abs(rel)
    max_exact = nb // 2
    nf = jnp.maximum(n, 1).astype(F32)
    large = max_exact + (jnp.log(nf / max_exact) / math.log(MAX_DISTANCE / max_exact) * (nb - max_exact)).astype(jnp.int32)
    large = jnp.minimum(large, nb - 1)
    return ret + jnp.where(n < max_exact, n, large)


def _bias_tables(rel_bias, tq):
    assert tq >= MAX_DISTANCE
    m = jnp.arange(2 * tq - 1, dtype=jnp.int32)
    rel = jnp.stack([d * tq + m - (tq - 1) for d in range(-BIAS_REACH, BIAS_REACH + 1)])
    w = jnp.transpose(rel_bias[_rel_bucket(rel)], (2, 0, 1)).astype(F32) * LOG2E
    period = 2 * tq
    v = jnp.concatenate([w[..., tq - 1::-1], jnp.zeros(w.shape[:2] + (1,), F32), w[..., :tq - 1:-1]], axis=-1)
    rep = jnp.tile(v, (1, 1, tq))[..., :tq * (period - 1)]
    return rep.reshape(w.shape[:2] + (tq, period - 1))[..., :tq]


def kernel(x_prompt, x_sample, norm1_g, w_in, b_gate, conv_w, conv_b, lru_wa_f, lru_ba_f, lru_wx_f, lru_bx_f, lru_lam_f, lru_wa_b, lru_ba_b, lru_wx_b, lru_bx_b, lru_lam_b, lam_q1, lam_k1, lam_q2, lam_k2, subln_g, rel_bias, w_rnn_out, w_attn_out, w_out, norm2_g, peer_wq, peer_keys, peer_u, peer_v, final_g):
    groups = [x_prompt.shape[:2], x_sample.shape[:2]]
    x2d = jnp.concatenate([x_prompt.reshape(-1, D_MODEL), x_sample.reshape(-1, D_MODEL)], axis=0)
    t = x2d.shape[0]
    l = 0
    row = lambda a: a.reshape(1, -1)

    w_main = jnp.concatenate([w_in[l][:, :QKV_COL0], w_in[l][:, QKV_COL1:]], axis=1).astype(BF16)
    proj = _inproj(x2d, row(norm1_g[l]), w_main)
    qk, vt = _qkv(x2d, row(norm1_g[l]), w_in[l][:, QKV_COL0:QKV_COL1].astype(BF16))

    btiles = _bias_tables(rel_bias, 512)
    lamv = jnp.stack([lam_q1[l], lam_k1[l], lam_q2[l], lam_k2[l]])
    lru_f = (conv_w[l], row(conv_b[l]), lru_wa_f[l].astype(BF16), row(lru_ba_f[l]),
             lru_wx_f[l].astype(BF16), row(lru_bx_f[l]), row(lru_lam_f[l]))
    lru_b = (conv_w[l], row(conv_b[l]), lru_wa_b[l].astype(BF16), row(lru_ba_b[l]),
             lru_wx_b[l].astype(BF16), row(lru_bx_b[l]), row(lru_lam_b[l]))
    hf, hb, attn = [], [], []
    row0 = 0
    for nb, s in groups:
        hf.append(_lru(proj, row0, nb, s, *lru_f, reverse=False))
        hb.append(_lru(proj, row0, nb, s, *lru_b, reverse=True))
        attn.append(_attention(qk, vt, row0, nb, s, btiles, lamv, row(subln_g[l])))
        row0 += nb * s

    x1, xn2, qp = _merge(x2d, hf, hb, proj, attn, b_gate[l].reshape(2, D_MODEL),
                         w_rnn_out[l].astype(BF16), w_attn_out[l].astype(BF16), w_out[l].astype(BF16),
                         row(norm2_g[l]), peer_wq[l].astype(BF16))

    idx, g = _router(qp, peer_keys[l].astype(BF16))
    idx_t = idx.T
    g_t = g.T
    tab_u = _pack_table(peer_u[l].reshape(N_EXPERTS * SUBLANES, LANES))
    tab_v = _pack_table(peer_v[l].reshape(N_EXPERTS * SUBLANES, LANES))
    w = _peer_u(idx_t, xn2.reshape(t, SUBLANES, LANES), g_t, tab_u)
    o = _peer_v(idx_t, w, tab_v).reshape(t, D_MODEL)

    y = _final(x1, o, row(final_g))
    n0 = groups[0][0] * groups[0][1]
    return (y[:n0].reshape(x_prompt.shape), y[n0:].reshape(x_sample.shape))
```

```python
import functools
import math

import jax
import jax.numpy as jnp
from jax import lax
from jax.experimental import pallas as pl
from jax.experimental.pallas import tpu as pltpu

F32 = jnp.float32
BF16 = jnp.bfloat16

D_MODEL = 1024
D_RNN = 1024
LRU_BLOCKS = 4
LRU_BW = D_RNN // LRU_BLOCKS
LRU_C = 8.0
CONV_W = 4
N_HEADS = 8
HEAD_DIM = 64
V_DIM = 2 * HEAD_DIM
V_AUG = V_DIM + 16
ATTN_W = N_HEADS * V_DIM
NUM_BUCKETS = 32
MAX_DISTANCE = 128
PEER_HEADS = 8
N_KEYS = 128
N_EXPERTS = N_KEYS * N_KEYS
PEER_TOPK = 16
D_KEY = 256
D_HALF = D_KEY // 2
N_SEL = PEER_HEADS * PEER_TOPK
IN_W = 2 * D_RNN + 3 * ATTN_W + 2 * D_MODEL
EPS = 1e-6
LAM_INIT = 0.8 - 0.6 * math.exp(-0.3 * 0)
LOG2E = math.log2(math.e)
NT_DIMS = (((1,), (1,)), ((), ()))
BIAS_REACH = 2

COL_XRNN, COL_GRNN, COL_GLR, COL_GLA = range(4)
QKV_COL0 = 2 * D_RNN
QKV_COL1 = QKV_COL0 + 3 * ATTN_W

SUBLANES = 8
LANES = 128
ROW_WORDS = D_MODEL // 2
ROW_SUB = ROW_WORDS // LANES
VMEM_LIMIT = 56 * 1024 * 1024


def _rms(x, g):
    return x * lax.rsqrt(jnp.mean(x * x, axis=-1, keepdims=True) + EPS) * g


def _group_rows(tm, nblk0, nblk1):
    return (lambda i, *_: (jnp.minimum(i, nblk0 - 1), 0),
            lambda i, *_: (jnp.clip(i - nblk0, 0, nblk1 - 1), 0))


def _inproj_body(x0_ref, x1_ref, g_ref, w_ref, o_ref, xn_ref, *, nblk0):
    @pl.when(pl.program_id(1) == 0)
    def _():
        x = jnp.where(pl.program_id(0) < nblk0, x0_ref[...], x1_ref[...])
        xn_ref[...] = _rms(x, g_ref[...]).astype(BF16)

    o_ref[...] = jnp.dot(xn_ref[...], w_ref[...], preferred_element_type=F32)


def _inproj(xs, g, w):
    n_out = w.shape[1]
    tm, tn = 1024, 1024
    nblk0, nblk1 = xs[0].shape[0] // tm, xs[1].shape[0] // tm
    rows0, rows1 = _group_rows(tm, nblk0, nblk1)
    return pl.pallas_call(
        functools.partial(_inproj_body, nblk0=nblk0),
        grid=(nblk0 + nblk1, n_out // tn),
        in_specs=[
            pl.BlockSpec((tm, D_MODEL), rows0),
            pl.BlockSpec((tm, D_MODEL), rows1),
            pl.BlockSpec((1, D_MODEL), lambda i, j: (0, 0)),
            pl.BlockSpec((D_MODEL, tn), lambda i, j: (0, j)),
        ],
        out_specs=pl.BlockSpec((tm, tn), lambda i, j: (i, j)),
        out_shape=jax.ShapeDtypeStruct(((nblk0 + nblk1) * tm, n_out), F32),
        scratch_shapes=[pltpu.VMEM((tm, D_MODEL), BF16)],
        compiler_params=pltpu.CompilerParams(dimension_semantics=("parallel", "arbitrary")),
        name="inproj",
    )(xs[0], xs[1], g, w)


def _qkv_body(x0_ref, x1_ref, g_ref, w_ref, qk_ref, vt_ref, xn_ref, *, nblk0):
    j = pl.program_id(1)

    @pl.when(j == 0)
    def _():
        x = jnp.where(pl.program_id(0) < nblk0, x0_ref[...], x1_ref[...])
        xn_ref[...] = _rms(x, g_ref[...]).astype(BF16)

    res = jnp.dot(xn_ref[...], w_ref[...], preferred_element_type=F32)

    @pl.when(j < 2)
    def _():
        for h in range(N_HEADS):
            qk_ref[h] = res[:, h * V_DIM:(h + 1) * V_DIM].astype(BF16)

    @pl.when(j == 2)
    def _():
        for h in range(N_HEADS):
            vt_ref[h, 0:V_DIM, :] = res[:, h * V_DIM:(h + 1) * V_DIM].T.astype(BF16)
            vt_ref[h, V_DIM:V_AUG, :] = jnp.ones((V_AUG - V_DIM, res.shape[0]), BF16)


def _qkv(xs, g, w):
    tm = 1024
    nblk0, nblk1 = xs[0].shape[0] // tm, xs[1].shape[0] // tm
    t = (nblk0 + nblk1) * tm
    rows0, rows1 = _group_rows(tm, nblk0, nblk1)
    return pl.pallas_call(
        functools.partial(_qkv_body, nblk0=nblk0),
        grid=(nblk0 + nblk1, 3),
        in_specs=[
            pl.BlockSpec((tm, D_MODEL), rows0),
            pl.BlockSpec((tm, D_MODEL), rows1),
            pl.BlockSpec((1, D_MODEL), lambda i, j: (0, 0)),
            pl.BlockSpec((D_MODEL, ATTN_W), lambda i, j: (0, j)),
        ],
        out_specs=[pl.BlockSpec((None, N_HEADS, tm, V_DIM), lambda i, j: (jnp.minimum(j, 1), 0, i, 0)),
                   pl.BlockSpec((N_HEADS, V_AUG, tm), lambda i, j: (0, 0, i))],
        out_shape=[jax.ShapeDtypeStruct((2, N_HEADS, t, V_DIM), BF16),
                   jax.ShapeDtypeStruct((N_HEADS, V_AUG, t), BF16)],
        scratch_shapes=[pltpu.VMEM((tm, D_MODEL), BF16)],
        compiler_params=pltpu.CompilerParams(dimension_semantics=("parallel", "arbitrary")),
        name="qkv_proj",
    )(xs[0], xs[1], g, w)


def _lru_body(xc_ref, xp_ref, xn_ref, cw_ref, cb_ref, wa_ref, ba_ref, wx_ref, bx_ref, lam_ref,
              h_ref, a_s, b_s, carry, *, reverse, nchunks, tc):
    i = pl.program_id(1)
    c = (nchunks - 1 - i) if reverse else i
    x = xc_ref[...]
    rows = lax.broadcasted_iota(jnp.int32, (tc, D_RNN), 0)
    prev = jnp.where(c > 0, xp_ref[SUBLANES - 1:SUBLANES, :], 0.0)
    nxt0 = jnp.where(c < nchunks - 1, xn_ref[0:1, :], 0.0)
    nxt1 = jnp.where(c < nchunks - 1, xn_ref[1:2, :], 0.0)
    xm1 = jnp.where(rows == 0, prev, pltpu.roll(x, 1, 0))
    xp1 = jnp.where(rows == tc - 1, nxt0, pltpu.roll(x, tc - 1, 0))
    xp2 = jnp.where(rows == tc - 2, nxt0, jnp.where(rows == tc - 1, nxt1, pltpu.roll(x, tc - 2, 0)))
    cw = cw_ref[...]
    xc = cb_ref[...] + xm1 * cw[0:1] + x * cw[1:2] + xp1 * cw[2:3] + xp2 * cw[3:4]

    xcb = xc.astype(BF16)
    r_parts, i_parts = [], []
    for n in range(LRU_BLOCKS):
        xs = xcb[:, n * LRU_BW:(n + 1) * LRU_BW]
        r_parts.append(jnp.dot(xs, wa_ref[n], preferred_element_type=F32))
        i_parts.append(jnp.dot(xs, wx_ref[n], preferred_element_type=F32))
    r = jax.nn.sigmoid(jnp.concatenate(r_parts, axis=1) + ba_ref[...])
    ig = jax.nn.sigmoid(jnp.concatenate(i_parts, axis=1) + bx_ref[...])
    z = -lam_ref[...]
    softplus = jnp.maximum(z, 0.0) + jnp.log1p(jnp.exp(-jnp.abs(z)))
    log_a = -LRU_C * r * softplus
    a = jnp.exp(log_a)
    mult = jnp.sqrt(-jnp.tanh(log_a) * (a * a + 1.0))
    edge_row = tc - 1 if reverse else 0
    edge_chunk = nchunks - 1 if reverse else 0
    mult = jnp.where(rows == jnp.where(c == edge_chunk, edge_row, -1), 1.0, mult)
    a_s[...] = a
    b_s[...] = mult * ig * xc

    row8 = lax.broadcasted_iota(jnp.int32, (SUBLANES, D_RNN), 0)
    ngroups = tc // SUBLANES
    h0 = jnp.where(i == 0, 0.0, carry[...])

    def step(gi, h):
        g = (ngroups - 1 - gi) if reverse else gi
        off = pl.multiple_of(g * SUBLANES, SUBLANES)
        av = a_s[pl.ds(off, SUBLANES), :]
        bv = b_s[pl.ds(off, SUBLANES), :]
        for s in (1, 2, 4):
            if reverse:
                keep = row8 < SUBLANES - s
                shift = SUBLANES - s
            else:
                keep = row8 >= s
                shift = s
            a_sh = jnp.where(keep, pltpu.roll(av, shift, 0), 1.0)
            b_sh = jnp.where(keep, pltpu.roll(bv, shift, 0), 0.0)
            bv = av * b_sh + bv
            av = av * a_sh
        hv = av * h + bv
        h_ref[pl.ds(off, SUBLANES), :] = hv
        last = hv[0:1, :] if reverse else hv[SUBLANES - 1:SUBLANES, :]
        return jnp.broadcast_to(last, (SUBLANES, D_RNN))

    carry[...] = lax.fori_loop(0, ngroups, step, h0)


def _lru(proj, row0, nb, s, cw, cb, wa, ba, wx, bx, lam, reverse):
    tc = 512
    nchunks = s // tc
    t_all = proj.shape[0]
    blk0 = row0 // tc
    last8 = t_all // SUBLANES - 1

    def cidx(i):
        return (nchunks - 1 - i) if reverse else i

    def cur(b, i):
        return (blk0 + b * nchunks + cidx(i), COL_XRNN)

    def prev8(b, i):
        r = (blk0 + b * nchunks + cidx(i)) * (tc // SUBLANES) - 1
        return (jnp.maximum(r, 0), COL_XRNN)

    def next8(b, i):
        r = (blk0 + b * nchunks + cidx(i) + 1) * (tc // SUBLANES)
        return (jnp.minimum(r, last8), COL_XRNN)

    full = lambda shape: pl.BlockSpec(shape, lambda b, i: (0,) * len(shape))
    body = functools.partial(_lru_body, reverse=reverse, nchunks=nchunks, tc=tc)
    return pl.pallas_call(
        body,
        grid=(nb, nchunks),
        in_specs=[
            pl.BlockSpec((tc, D_RNN), cur),
            pl.BlockSpec((SUBLANES, D_RNN), prev8),
            pl.BlockSpec((SUBLANES, D_RNN), next8),
            full((CONV_W, D_RNN)), full((1, D_RNN)),
            full((LRU_BLOCKS, LRU_BW, LRU_BW)), full((1, D_RNN)),
            full((LRU_BLOCKS, LRU_BW, LRU_BW)), full((1, D_RNN)),
            full((1, D_RNN)),
        ],
        out_specs=pl.BlockSpec((tc, D_RNN), lambda b, i: (b * nchunks + cidx(i), 0)),
        out_shape=jax.ShapeDtypeStruct((nb * s, D_RNN), F32),
        scratch_shapes=[pltpu.VMEM((tc, D_RNN), F32), pltpu.VMEM((tc, D_RNN), F32),
                        pltpu.VMEM((SUBLANES, D_RNN), F32)],
        compiler_params=pltpu.CompilerParams(dimension_semantics=("arbitrary", "arbitrary")),
        name="lru_bwd" if reverse else "lru_fwd",
    )(proj, proj, proj, cw, cb, wa, ba, wx, bx, lam)


def _attn_body(q_ref, k_ref, vt_ref, bt_ref, lamv_ref, g_ref, o_ref,
               q1_s, q2_s, m1_s, acc1_s, m2_s, acc2_s, sa_s, sb_s, *, nkv, tkv):
    i = pl.program_id(2)

    q = q_ref[...].astype(F32) * (HEAD_DIM ** -0.5 * LOG2E)
    lane = lax.broadcasted_iota(jnp.int32, q.shape, 1)
    q1_s[...] = jnp.where(lane < HEAD_DIM, q, 0.0).astype(BF16)
    q2_s[...] = jnp.where(lane >= HEAD_DIM, q, 0.0).astype(BF16)
    for m_s, acc_s in ((m1_s, acc1_s), (m2_s, acc2_s)):
        m_s[...] = jnp.full(m_s.shape, -jnp.inf, F32)
        acc_s[...] = jnp.zeros(acc_s.shape, F32)

    def scores(j, buf):
        off = pl.multiple_of(j * tkv, tkv)
        k = k_ref[pl.ds(off, tkv), :]
        bias = bt_ref[jnp.clip(j - i, -BIAS_REACH, BIAS_REACH) + BIAS_REACH]
        for c, q_s in enumerate((q1_s, q2_s)):
            buf[c] = lax.dot_general(k, q_s[...], NT_DIMS, preferred_element_type=F32) + bias

    def accumulate(j, buf):
        off = pl.multiple_of(j * tkv, tkv)
        vt = vt_ref[:, pl.ds(off, tkv)]
        for c, (m_s, acc_s) in enumerate(((m1_s, acc1_s), (m2_s, acc2_s))):
            s = buf[c]
            m_old = m_s[...]
            m_new = jnp.maximum(m_old, jnp.max(s, axis=0, keepdims=True))
            alpha = jnp.exp2(m_old - m_new)
            p = jnp.exp2((s - m_new).astype(BF16))
            acc_s[...] = alpha * acc_s[...] + jnp.dot(vt, p, preferred_element_type=F32)
            m_s[...] = m_new

    scores(0, sa_s)

    def pair(u, carry):
        j = 2 * u
        scores(j + 1, sb_s)
        accumulate(j, sa_s)
        scores(j + 2, sa_s)
        accumulate(j + 1, sb_s)
        return carry

    lax.fori_loop(0, nkv // 2 - 1, pair, 0)
    scores(nkv - 1, sb_s)
    accumulate(nkv - 2, sa_s)
    accumulate(nkv - 1, sb_s)

    lv = lamv_ref[...]
    lam = (jnp.exp(jnp.sum(lv[0:1] * lv[1:2], axis=-1, keepdims=True))
           - jnp.exp(jnp.sum(lv[2:3] * lv[3:4], axis=-1, keepdims=True)) + LAM_INIT)
    o = (acc1_s[0:V_DIM, :] / acc1_s[V_DIM:V_DIM + 1, :]
         - lam * (acc2_s[0:V_DIM, :] / acc2_s[V_DIM:V_DIM + 1, :]))
    o_ref[...] = _rms(o.T, g_ref[...]) * (1.0 - LAM_INIT)


def _attention(qk, vt, row0, nb, s, btiles, lamv, subln_g):
    tq = btiles.shape[-1]
    nq = s // tq
    assert nq >= 2 and nq % 2 == 0
    blk0 = row0 // tq
    seq0 = row0 // s

    body = functools.partial(_attn_body, nkv=nq, tkv=tq)
    stat = pltpu.VMEM((1, tq), F32)
    acc = pltpu.VMEM((V_AUG, tq), F32)
    sbuf = pltpu.VMEM((2, tq, tq), F32)
    return pl.pallas_call(
        body,
        grid=(nb, N_HEADS, nq),
        in_specs=[
            pl.BlockSpec((None, None, tq, V_DIM), lambda b, h, i: (0, h, blk0 + b * nq + i, 0)),
            pl.BlockSpec((None, None, s, V_DIM), lambda b, h, i: (1, h, seq0 + b, 0)),
            pl.BlockSpec((None, V_AUG, s), lambda b, h, i: (h, 0, seq0 + b)),
            pl.BlockSpec((None, 2 * BIAS_REACH + 1, tq, tq), lambda b, h, i: (h, 0, 0, 0)),
            pl.BlockSpec((4, HEAD_DIM), lambda b, h, i: (0, 0)),
            pl.BlockSpec((1, V_DIM), lambda b, h, i: (0, 0)),
        ],
        out_specs=pl.BlockSpec((tq, V_DIM), lambda b, h, i: (b * nq + i, h)),
        out_shape=jax.ShapeDtypeStruct((nb * s, ATTN_W), F32),
        scratch_shapes=[pltpu.VMEM((tq, V_DIM), BF16), pltpu.VMEM((tq, V_DIM), BF16),
                        stat, acc, stat, acc, sbuf, sbuf],
        compiler_params=pltpu.CompilerParams(
            dimension_semantics=("parallel", "parallel", "arbitrary"),
            vmem_limit_bytes=VMEM_LIMIT),
        name="diff_attn",
    )(qk, qk, vt, btiles, lamv, subln_g)


def _merge_body(xg0_ref, xg1_ref, hf0_ref, hf1_ref, hb0_ref, hb1_ref, at0_ref, at1_ref, grnn_ref, glr_ref, gla_ref, bg_ref,
                wr_ref, wa_ref, wo_ref, n2_ref, wq_ref, x1_ref, xn2_ref, qp_ref, *, nblk0):
    first = pl.program_id(0) < nblk0
    x = jnp.where(first, xg0_ref[...], xg1_ref[...])
    hf = jnp.where(first, hf0_ref[...], hf1_ref[...])
    hb = jnp.where(first, hb0_ref[...], hb1_ref[...])
    attn = jnp.where(first, at0_ref[...], at1_ref[...])
    hg = ((hf + hb) * jax.nn.gelu(grnn_ref[...])).astype(BF16)
    y_rnn = jnp.dot(hg, wr_ref[...], preferred_element_type=F32)
    y_attn = jnp.dot(attn.astype(BF16), wa_ref[...], preferred_element_type=F32)
    bg = bg_ref[...]
    g_r = jax.nn.sigmoid(glr_ref[...] + bg[0:1])
    g_a = jax.nn.sigmoid(gla_ref[...] + bg[1:2])
    merged = (g_r * y_rnn + g_a * y_attn).astype(BF16)
    x1 = x + jnp.dot(merged, wo_ref[...], preferred_element_type=F32)
    x1_ref[...] = x1
    xn2 = _rms(x1, n2_ref[...])
    xn2_ref[...] = xn2
    qp_ref[...] = jnp.dot(xn2.astype(BF16), wq_ref[...], preferred_element_type=F32)


def _merge(xs, hf, hb, proj, attn, bg, wr, wa, wo, n2, wq):
    tm = 256
    nblk0 = hf[0].shape[0] // tm
    nblk1 = hf[1].shape[0] // tm
    t = (nblk0 + nblk1) * tm
    row = lambda c: pl.BlockSpec((tm, D_MODEL), lambda i: (i, c))
    rows0, rows1 = _group_rows(tm, nblk0, nblk1)
    grp0 = pl.BlockSpec((tm, D_MODEL), rows0)
    grp1 = pl.BlockSpec((tm, D_MODEL), rows1)
    full = lambda shape: pl.BlockSpec(shape, lambda i: (0,) * len(shape))
    qw = PEER_HEADS * D_KEY
    return pl.pallas_call(
        functools.partial(_merge_body, nblk0=nblk0),
        grid=(t // tm,),
        in_specs=[grp0, grp1, grp0, grp1, grp0, grp1, grp0, grp1, row(COL_GRNN), row(COL_GLR), row(COL_GLA),
                  full((2, D_MODEL)), full((D_RNN, D_MODEL)), full((ATTN_W, D_MODEL)),
                  full((D_MODEL, D_MODEL)), full((1, D_MODEL)), full((D_MODEL, qw))],
        out_specs=[row(0), row(0), pl.BlockSpec((tm, qw), lambda i: (i, 0))],
        out_shape=[jax.ShapeDtypeStruct((t, D_MODEL), F32), jax.ShapeDtypeStruct((t, D_MODEL), F32),
                   jax.ShapeDtypeStruct((t, qw), F32)],
        compiler_params=pltpu.CompilerParams(dimension_semantics=("parallel",),
                                             vmem_limit_bytes=VMEM_LIMIT),
        name="merge_proj",
    )(xs[0], xs[1], hf[0], hf[1], hb[0], hb[1], attn[0], attn[1], proj, proj, proj, bg, wr, wa, wo, n2, wq)


def _topk_rows(s, k, ids=None):
    if ids is None:
        ids = lax.broadcasted_iota(jnp.int32, s.shape, 0).astype(F32)
    slot = lax.broadcasted_iota(jnp.int32, (k, s.shape[1]), 0)
    vals = jnp.zeros((k, s.shape[1]), F32)
    idxs = jnp.zeros((k, s.shape[1]), F32)
    for t in range(k):
        m = jnp.max(s, axis=0, keepdims=True)
        pos = jnp.min(jnp.where(s == m, ids, jnp.inf), axis=0, keepdims=True)
        vals = jnp.where(slot == t, m, vals)
        idxs = jnp.where(slot == t, pos, idxs)
        s = jnp.where(ids == pos, -jnp.inf, s)
    return vals, idxs


_PAIRS = [(a, b) for a in range(PEER_TOPK) for b in range(PEER_TOPK) if (a + 1) * (b + 1) <= PEER_TOPK]


def _rows_from(src, picks, fill):
    if all(p is not None for p in picks) and picks[0] % SUBLANES == 0 and picks == list(range(picks[0], picks[0] + SUBLANES)):
        return src[picks[0]:picks[0] + SUBLANES, :]
    row = lax.broadcasted_iota(jnp.int32, (SUBLANES, src.shape[1]), 0)
    out = jnp.full((SUBLANES, src.shape[1]), fill, F32)
    for r, p in enumerate(picks):
        if p is not None:
            out = jnp.where(row == r, src[p:p + 1, :], out)
    return out


def _pair_candidates(sv1, sv2):
    pairs = _PAIRS + [None] * (-len(_PAIRS) % SUBLANES)
    width = sv1.shape[1]
    row = lax.broadcasted_iota(jnp.int32, (SUBLANES, width), 0)
    sums, ids = [], []
    for g0 in range(0, len(pairs), SUBLANES):
        grp = pairs[g0:g0 + SUBLANES]
        a_rows = _rows_from(sv1, [None if pr is None else pr[0] for pr in grp], -jnp.inf)
        b_rows = _rows_from(sv2, [None if pr is None else pr[1] for pr in grp], 0.0)
        sums.append(a_rows + b_rows)
        idv = jnp.full((SUBLANES, width), float(PEER_TOPK * PEER_TOPK), F32)
        for r, pr in enumerate(grp):
            if pr is not None:
                idv = jnp.where(row == r, float(pr[0] * PEER_TOPK + pr[1]), idv)
        ids.append(idv)
    return jnp.concatenate(sums, axis=0), jnp.concatenate(ids, axis=0)


def _router_body(qp_ref, keys_ref, idx_ref, g_ref):
    slot = lax.broadcasted_iota(jnp.int32, (PEER_TOPK, qp_ref.shape[0]), 0)
    for h in range(PEER_HEADS):
        sv, si = [], []
        for p in range(2):
            c0 = (h * 2 + p) * D_HALF
            q = qp_ref[:, c0:c0 + D_HALF].astype(BF16)
            s = lax.dot_general(keys_ref[h, p], q, (((1,), (1,)), ((), ())), preferred_element_type=F32)
            v, ix = _topk_rows(s, PEER_TOPK)
            sv.append(v)
            si.append(ix)
        cand, cand_id = _pair_candidates(sv[0], sv[1])
        top_s, pos = _topk_rows(cand, PEER_TOPK, cand_id)
        pos = pos.astype(jnp.int32)
        pa = pos >> 4
        pb = pos & (PEER_TOPK - 1)
        idx = jnp.zeros(slot.shape, F32)
        for t in range(PEER_TOPK):
            i1 = jnp.sum(jnp.where(slot == pa[t:t + 1, :], si[0], 0.0), axis=0, keepdims=True)
            i2 = jnp.sum(jnp.where(slot == pb[t:t + 1, :], si[1], 0.0), axis=0, keepdims=True)
            idx = jnp.where(slot == t, (i1 * N_KEYS + i2) * ROW_SUB, idx)
        idx = idx.astype(jnp.int32)
        e = jnp.exp(top_s - jnp.max(top_s, axis=0, keepdims=True))
        g = e / jnp.sum(e, axis=0, keepdims=True)
        idx_ref[h * PEER_TOPK:(h + 1) * PEER_TOPK, :] = idx
        g_ref[h * PEER_TOPK:(h + 1) * PEER_TOPK, :] = g


def _router(qp, keys):
    t = qp.shape[0]
    c = 256
    qw = PEER_HEADS * D_KEY
    return pl.pallas_call(
        _router_body,
        grid=(t // c,),
        in_specs=[pl.BlockSpec((c, qw), lambda i: (i, 0)),
                  pl.BlockSpec((PEER_HEADS, 2, N_KEYS, D_HALF), lambda i: (0, 0, 0, 0))],
        out_specs=[pl.BlockSpec((N_SEL, c), lambda i: (0, i)), pl.BlockSpec((N_SEL, c), lambda i: (0, i))],
        out_shape=[jax.ShapeDtypeStruct((N_SEL, t), jnp.int32), jax.ShapeDtypeStruct((N_SEL, t), F32)],
        compiler_params=pltpu.CompilerParams(dimension_semantics=("parallel",)),
        name="peer_router",
    )(qp, keys)


def _pack_body(t_ref, o_ref):
    o_ref[...] = pltpu.bitcast(t_ref[...].astype(BF16), jnp.int32)


def _pack_table(tab2):
    n8 = tab2.shape[0]
    r = 4096
    return pl.pallas_call(
        _pack_body,
        grid=(n8 // r,),
        in_specs=[pl.BlockSpec((r, LANES), lambda i: (i, 0))],
        out_specs=pl.BlockSpec((r // 2, LANES), lambda i: (i, 0)),
        out_shape=jax.ShapeDtypeStruct((n8 // 2, LANES), jnp.int32),
        compiler_params=pltpu.CompilerParams(dimension_semantics=("parallel",)),
        name="peer_pack",
    )(tab2)


PEER_UNROLL = SUBLANES
PEER_U_BLOCK = 128
PEER_V_BLOCK = 256


def _load_table(tab_hbm, tab_s, sem):
    @pl.when(pl.program_id(0) == 0)
    def _():
        cp = pltpu.make_async_copy(tab_hbm, tab_s, sem)
        cp.start()
        cp.wait()


def _gather_rows(tab_s, idx_ref, t, m_ref):
    for j in range(N_SEL):
        r = pl.multiple_of(idx_ref[t, j], ROW_SUB)
        m_ref[j * ROW_SUB:(j + 1) * ROW_SUB, :] = tab_s[pl.ds(r, ROW_SUB), :]


def _split3_bf16(x):
    hi = x.astype(BF16).astype(F32)
    r1 = x - hi
    mid = r1.astype(BF16).astype(F32)
    lo = r1 - mid
    return jnp.concatenate([hi, mid, lo, jnp.zeros_like(x)], axis=0).astype(BF16)


def _sum3(y):
    return y[0:SUBLANES] + y[SUBLANES:2 * SUBLANES] + y[2 * SUBLANES:3 * SUBLANES]


def _diag_mask():
    p = lax.broadcasted_iota(jnp.int32, (SUBLANES, N_SEL * SUBLANES), 0)
    c = lax.broadcasted_iota(jnp.int32, (SUBLANES, N_SEL * SUBLANES), 1)
    return (c & (SUBLANES - 1)) == p


def _pipelined_groups(ngroups, gather, compute):
    assert ngroups >= 2 and ngroups % 2 == 0
    gather(0, 0)

    def pair(p, carry):
        g = 2 * p
        gather(g + 1, 1)
        compute(g, 0)
        gather(g + 2, 0)
        compute(g + 1, 1)
        return carry

    lax.fori_loop(0, ngroups // 2 - 1, pair, 0)
    gather(ngroups - 1, 1)
    compute(ngroups - 2, 0)
    compute(ngroups - 1, 1)


def _peer_u_body(idx_ref, x_ref, g_ref, tab_hbm, w_ref, tab_s, m_s, fold_s, p_s, s_s, sem, *, tb):
    _load_table(tab_hbm, tab_s, sem)

    @pl.when(pl.program_id(0) == 0)
    def _():
        r = lax.broadcasted_iota(jnp.int32, fold_s.shape, 0)
        c = lax.broadcasted_iota(jnp.int32, fold_s.shape, 1)
        fold_s[...] = jnp.where((r >> 3) == c, 1.0, 0.0).astype(BF16)

    diag = _diag_mask()

    def gather(gi, half):
        for u in range(PEER_UNROLL):
            _gather_rows(tab_s, idx_ref, gi * PEER_UNROLL + u, m_s.at[half * PEER_UNROLL + u])

    def compute(gi, half):
        for u in range(PEER_UNROLL):
            t = gi * PEER_UNROLL + u
            rows = pltpu.bitcast(m_s[half * PEER_UNROLL + u], BF16)
            p = _sum3(lax.dot_general(_split3_bf16(x_ref[t]), rows, NT_DIMS, preferred_element_type=F32))
            p_s[t] = jnp.where(diag, p, 0.0)

    _pipelined_groups(tb // PEER_UNROLL, gather, compute)

    pm = p_s[...].reshape(tb * SUBLANES, N_SEL * SUBLANES)
    hi = pm.astype(BF16)
    lo = (pm - hi.astype(F32)).astype(BF16)
    fold = fold_s[...]
    y = jnp.dot(hi, fold, preferred_element_type=F32) + jnp.dot(lo, fold, preferred_element_type=F32)
    s_s[...] = y.reshape(tb, SUBLANES, N_SEL)
    s = s_s[:, 0, :]
    for q in range(1, SUBLANES):
        s = s + s_s[:, q, :]
    w_ref[...] = g_ref[...] * jax.nn.gelu(s)


def _peer_u(idx_t, x3, g_t, tab):
    t = idx_t.shape[0]
    tb = PEER_U_BLOCK
    body = functools.partial(_peer_u_body, tb=tb)
    return pl.pallas_call(
        body,
        grid=(t // tb,),
        in_specs=[pl.BlockSpec((tb, N_SEL), lambda i: (i, 0), memory_space=pltpu.SMEM),
                  pl.BlockSpec((tb, SUBLANES, LANES), lambda i: (i, 0, 0)),
                  pl.BlockSpec((tb, N_SEL), lambda i: (i, 0)),
                  pl.BlockSpec(memory_space=pl.ANY)],
        out_specs=pl.BlockSpec((tb, N_SEL), lambda i: (i, 0)),
        out_shape=jax.ShapeDtypeStruct((t, N_SEL), F32),
        scratch_shapes=[pltpu.VMEM((N_EXPERTS * ROW_SUB, LANES), jnp.int32),
                        pltpu.VMEM((2 * PEER_UNROLL, N_SEL * ROW_SUB, LANES), jnp.int32),
                        pltpu.VMEM((N_SEL * SUBLANES, N_SEL), BF16),
                        pltpu.VMEM((tb, SUBLANES, N_SEL * SUBLANES), F32),
                        pltpu.VMEM((tb, SUBLANES, LANES), F32),
                        pltpu.SemaphoreType.DMA(())],
        compiler_params=pltpu.CompilerParams(dimension_semantics=("arbitrary",),
                                             vmem_limit_bytes=VMEM_LIMIT),
        name="peer_u",
    )(idx_t, x3, g_t, tab)


def _peer_v_body(idx_ref, w_ref, tab_hbm, o_ref, tab_s, m_s, spread_s, sem, *, tb):
    _load_table(tab_hbm, tab_s, sem)

    @pl.when(pl.program_id(0) == 0)
    def _():
        r = lax.broadcasted_iota(jnp.int32, spread_s.shape, 0)
        c = lax.broadcasted_iota(jnp.int32, spread_s.shape, 1)
        spread_s[...] = jnp.where((c >> 3) == r, 1.0, 0.0).astype(BF16)

    diag = _diag_mask()
    zero = jnp.zeros((SUBLANES, N_SEL * SUBLANES), F32)

    def gather(gi, half):
        for u in range(PEER_UNROLL):
            _gather_rows(tab_s, idx_ref, gi * PEER_UNROLL + u, m_s.at[half * PEER_UNROLL + u])

    def compute(gi, half):
        base = pl.multiple_of(gi * PEER_UNROLL, PEER_UNROLL)
        e = jnp.dot(_split3_bf16(w_ref[pl.ds(base, PEER_UNROLL), :]), spread_s[...], preferred_element_type=F32)
        for u in range(PEER_UNROLL):
            rows = pltpu.bitcast(m_s[half * PEER_UNROLL + u], BF16)
            terms = [jnp.where(diag, jnp.broadcast_to(e[SUBLANES * k + u:SUBLANES * k + u + 1, :], diag.shape), 0.0)
                     for k in range(3)]
            lhs = jnp.concatenate(terms + [zero], axis=0).astype(BF16)
            o_ref[base + u] = _sum3(jnp.dot(lhs, rows, preferred_element_type=F32))

    _pipelined_groups(tb // PEER_UNROLL, gather, compute)


def _peer_v(idx_t, w, tab):
    t = idx_t.shape[0]
    tb = PEER_V_BLOCK
    body = functools.partial(_peer_v_body, tb=tb)
    return pl.pallas_call(
        body,
        grid=(t // tb,),
        in_specs=[pl.BlockSpec((tb, N_SEL), lambda i: (i, 0), memory_space=pltpu.SMEM),
                  pl.BlockSpec((tb, N_SEL), lambda i: (i, 0)),
                  pl.BlockSpec(memory_space=pl.ANY)],
        out_specs=pl.BlockSpec((tb, SUBLANES, LANES), lambda i: (i, 0, 0)),
        out_shape=jax.ShapeDtypeStruct((t, SUBLANES, LANES), F32),
        scratch_shapes=[pltpu.VMEM((N_EXPERTS * ROW_SUB, LANES), jnp.int32),
                        pltpu.VMEM((2 * PEER_UNROLL, N_SEL * ROW_SUB, LANES), jnp.int32),
                        pltpu.VMEM((N_SEL, N_SEL * SUBLANES), BF16),
                        pltpu.SemaphoreType.DMA(())],
        compiler_params=pltpu.CompilerParams(dimension_semantics=("arbitrary",),
                                             vmem_limit_bytes=VMEM_LIMIT),
        name="peer_v",
    )(idx_t, w, tab)


def _final_body(x1_ref, o_ref, g_ref, y_ref):
    y_ref[...] = _rms(x1_ref[...] + o_ref[...], g_ref[...])


def _final(x1, o, g):
    t = x1.shape[0]
    tm = 512
    row = pl.BlockSpec((tm, D_MODEL), lambda i: (i, 0))
    return pl.pallas_call(
        _final_body,
        grid=(t // tm,),
        in_specs=[row, row, pl.BlockSpec((1, D_MODEL), lambda i: (0, 0))],
        out_specs=row,
        out_shape=jax.ShapeDtypeStruct((t, D_MODEL), F32),
        compiler_params=pltpu.CompilerParams(dimension_semantics=("parallel",)),
        name="final_norm",
    )(x1, o, g)


def _rel_bucket(rel):
    nb = NUM_BUCKETS // 2
    ret = jnp.where(rel > 0, nb, 0).astype(jnp.int32)
    n = jnp.abs(rel)
    max_exact = nb // 2
    nf = jnp.maximum(n, 1).astype(F32)
    large = max_exact + (jnp.log(nf / max_exact) / math.log(MAX_DISTANCE / max_exact) * (nb - max_exact)).astype(jnp.int32)
    large = jnp.minimum(large, nb - 1)
    return ret + jnp.where(n < max_exact, n, large)


def _bias_tables(rel_bias, tq):
    assert tq >= MAX_DISTANCE
    m = jnp.arange(2 * tq - 1, dtype=jnp.int32)
    rel = jnp.stack([d * tq + m - (tq - 1) for d in range(-BIAS_REACH, BIAS_REACH + 1)])
    w = jnp.transpose(rel_bias[_rel_bucket(rel)], (2, 0, 1)).astype(F32) * LOG2E
    period = 2 * tq
    v = jnp.concatenate([w[..., tq - 1::-1], jnp.zeros(w.shape[:2] + (1,), F32), w[..., :tq - 1:-1]], axis=-1)
    rep = jnp.tile(v, (1, 1, tq))[..., :tq * (period - 1)]
    return rep.reshape(w.shape[:2] + (tq, period - 1))[..., :tq]


def kernel(x_prompt, x_sample, norm1_g, w_in, b_gate, conv_w, conv_b, lru_wa_f, lru_ba_f, lru_wx_f, lru_bx_f, lru_lam_f, lru_wa_b, lru_ba_b, lru_wx_b, lru_bx_b, lru_lam_b, lam_q1, lam_k1, lam_q2, lam_k2, subln_g, rel_bias, w_rnn_out, w_attn_out, w_out, norm2_g, peer_wq, peer_keys, peer_u, peer_v, final_g):
    groups = [x_prompt.shape[:2], x_sample.shape[:2]]
    xs = (x_prompt.reshape(-1, D_MODEL), x_sample.reshape(-1, D_MODEL))
    t = xs[0].shape[0] + xs[1].shape[0]
    l = 0
    row = lambda a: a.reshape(1, -1)

    w_main = jnp.concatenate([w_in[l][:, :QKV_COL0], w_in[l][:, QKV_COL1:]], axis=1).astype(BF16)
    proj = _inproj(xs, row(norm1_g[l]), w_main)
    qk, vt = _qkv(xs, row(norm1_g[l]), w_in[l][:, QKV_COL0:QKV_COL1].astype(BF16))

    btiles = _bias_tables(rel_bias, 512)
    lamv = jnp.stack([lam_q1[l], lam_k1[l], lam_q2[l], lam_k2[l]])
    lru_f = (conv_w[l], row(conv_b[l]), lru_wa_f[l].astype(BF16), row(lru_ba_f[l]),
             lru_wx_f[l].astype(BF16), row(lru_bx_f[l]), row(lru_lam_f[l]))
    lru_b = (conv_w[l], row(conv_b[l]), lru_wa_b[l].astype(BF16), row(lru_ba_b[l]),
             lru_wx_b[l].astype(BF16), row(lru_bx_b[l]), row(lru_lam_b[l]))
    hf, hb, attn = [], [], []
    row0 = 0
    for nb, s in groups:
        hf.append(_lru(proj, row0, nb, s, *lru_f, reverse=False))
        hb.append(_lru(proj, row0, nb, s, *lru_b, reverse=True))
        attn.append(_attention(qk, vt, row0, nb, s, btiles, lamv, row(subln_g[l])))
        row0 += nb * s

    x1, xn2, qp = _merge(xs, hf, hb, proj, attn, b_gate[l].reshape(2, D_MODEL),
                         w_rnn_out[l].astype(BF16), w_attn_out[l].astype(BF16), w_out[l].astype(BF16),
                         row(norm2_g[l]), peer_wq[l].astype(BF16))

    idx, g = _router(qp, peer_keys[l].astype(BF16))
    idx_t = idx.T
    g_t = g.T
    tab_u = _pack_table(peer_u[l].reshape(N_EXPERTS * SUBLANES, LANES))
    tab_v = _pack_table(peer_v[l].reshape(N_EXPERTS * SUBLANES, LANES))
    w = _peer_u(idx_t, xn2.reshape(t, SUBLANES, LANES), g_t, tab_u)
    o = _peer_v(idx_t, w, tab_v).reshape(t, D_MODEL)

    y = _final(x1, o, row(final_g))
    n0 = groups[0][0] * groups[0][1]
    return (y[:n0].reshape(x_prompt.shape), y[n0:].reshape(x_sample.shape))
```

```python
import functools
import math

import jax
import jax.numpy as jnp
from jax import lax
from jax.experimental import pallas as pl
from jax.experimental.pallas import tpu as pltpu

F32 = jnp.float32
BF16 = jnp.bfloat16

D_MODEL = 1024
D_RNN = 1024
LRU_BLOCKS = 4
LRU_BW = D_RNN // LRU_BLOCKS
LRU_C = 8.0
CONV_W = 4
N_HEADS = 8
HEAD_DIM = 64
V_DIM = 2 * HEAD_DIM
V_AUG = V_DIM + 16
ATTN_W = N_HEADS * V_DIM
NUM_BUCKETS = 32
MAX_DISTANCE = 128
PEER_HEADS = 8
N_KEYS = 128
N_EXPERTS = N_KEYS * N_KEYS
PEER_TOPK = 16
D_KEY = 256
D_HALF = D_KEY // 2
N_SEL = PEER_HEADS * PEER_TOPK
IN_W = 2 * D_RNN + 3 * ATTN_W + 2 * D_MODEL
EPS = 1e-6
LAM_INIT = 0.8 - 0.6 * math.exp(-0.3 * 0)
LOG2E = math.log2(math.e)
NT_DIMS = (((1,), (1,)), ((), ()))
BIAS_REACH = 2

COL_XRNN, COL_GRNN, COL_GLR, COL_GLA = range(4)
QKV_COL0 = 2 * D_RNN
QKV_COL1 = QKV_COL0 + 3 * ATTN_W

SUBLANES = 8
LANES = 128
ROW_WORDS = D_MODEL // 2
ROW_SUB = ROW_WORDS // LANES
VMEM_LIMIT = 56 * 1024 * 1024


def _rms(x, g):
    return x * lax.rsqrt(jnp.mean(x * x, axis=-1, keepdims=True) + EPS) * g


def _group_rows(tm, nblk0, nblk1):
    return (lambda i, *_: (jnp.minimum(i, nblk0 - 1), 0),
            lambda i, *_: (jnp.clip(i - nblk0, 0, nblk1 - 1), 0))


def _inproj_body(x0_ref, x1_ref, g_ref, w_ref, o_ref, xn_ref, *, nblk0):
    @pl.when(pl.program_id(1) == 0)
    def _():
        x = jnp.where(pl.program_id(0) < nblk0, x0_ref[...], x1_ref[...])
        xn_ref[...] = _rms(x, g_ref[...]).astype(BF16)

    o_ref[...] = jnp.dot(xn_ref[...], w_ref[...], preferred_element_type=F32)


def _inproj(xs, g, w):
    n_out = w.shape[1]
    tm, tn = 1024, 1024
    nblk0, nblk1 = xs[0].shape[0] // tm, xs[1].shape[0] // tm
    rows0, rows1 = _group_rows(tm, nblk0, nblk1)
    return pl.pallas_call(
        functools.partial(_inproj_body, nblk0=nblk0),
        grid=(nblk0 + nblk1, n_out // tn),
        in_specs=[
            pl.BlockSpec((tm, D_MODEL), rows0),
            pl.BlockSpec((tm, D_MODEL), rows1),
            pl.BlockSpec((1, D_MODEL), lambda i, j: (0, 0)),
            pl.BlockSpec((D_MODEL, tn), lambda i, j: (0, j)),
        ],
        out_specs=pl.BlockSpec((tm, tn), lambda i, j: (i, j)),
        out_shape=jax.ShapeDtypeStruct(((nblk0 + nblk1) * tm, n_out), F32),
        scratch_shapes=[pltpu.VMEM((tm, D_MODEL), BF16)],
        compiler_params=pltpu.CompilerParams(dimension_semantics=("parallel", "arbitrary")),
        name="inproj",
    )(xs[0], xs[1], g, w)


def _qkv_body(x0_ref, x1_ref, g_ref, w_ref, qk_ref, vt_ref, xn_ref, *, nblk0):
    j = pl.program_id(1)

    @pl.when(j == 0)
    def _():
        x = jnp.where(pl.program_id(0) < nblk0, x0_ref[...], x1_ref[...])
        xn_ref[...] = _rms(x, g_ref[...]).astype(BF16)

    res = jnp.dot(xn_ref[...], w_ref[...], preferred_element_type=F32)

    @pl.when(j < 2)
    def _():
        for h in range(N_HEADS):
            qk_ref[h] = res[:, h * V_DIM:(h + 1) * V_DIM].astype(BF16)

    @pl.when(j == 2)
    def _():
        for h in range(N_HEADS):
            vt_ref[h, 0:V_DIM, :] = res[:, h * V_DIM:(h + 1) * V_DIM].T.astype(BF16)
            vt_ref[h, V_DIM:V_AUG, :] = jnp.ones((V_AUG - V_DIM, res.shape[0]), BF16)


def _qkv(xs, g, w):
    tm = 1024
    nblk0, nblk1 = xs[0].shape[0] // tm, xs[1].shape[0] // tm
    t = (nblk0 + nblk1) * tm
    rows0, rows1 = _group_rows(tm, nblk0, nblk1)
    return pl.pallas_call(
        functools.partial(_qkv_body, nblk0=nblk0),
        grid=(nblk0 + nblk1, 3),
        in_specs=[
            pl.BlockSpec((tm, D_MODEL), rows0),
            pl.BlockSpec((tm, D_MODEL), rows1),
            pl.BlockSpec((1, D_MODEL), lambda i, j: (0, 0)),
            pl.BlockSpec((D_MODEL, ATTN_W), lambda i, j: (0, j)),
        ],
        out_specs=[pl.BlockSpec((None, N_HEADS, tm, V_DIM), lambda i, j: (jnp.minimum(j, 1), 0, i, 0)),
                   pl.BlockSpec((N_HEADS, V_AUG, tm), lambda i, j: (0, 0, i))],
        out_shape=[jax.ShapeDtypeStruct((2, N_HEADS, t, V_DIM), BF16),
                   jax.ShapeDtypeStruct((N_HEADS, V_AUG, t), BF16)],
        scratch_shapes=[pltpu.VMEM((tm, D_MODEL), BF16)],
        compiler_params=pltpu.CompilerParams(dimension_semantics=("parallel", "arbitrary")),
        name="qkv_proj",
    )(xs[0], xs[1], g, w)


def _lru_body(xc_ref, xp_ref, xn_ref, cw_ref, cb_ref, wa_ref, ba_ref, wx_ref, bx_ref, lam_ref,
              h_ref, a_s, b_s, carry, *, reverse, nchunks, tc):
    i = pl.program_id(1)
    c = (nchunks - 1 - i) if reverse else i
    x = xc_ref[...]
    rows = lax.broadcasted_iota(jnp.int32, (tc, D_RNN), 0)
    prev = jnp.where(c > 0, xp_ref[SUBLANES - 1:SUBLANES, :], 0.0)
    nxt0 = jnp.where(c < nchunks - 1, xn_ref[0:1, :], 0.0)
    nxt1 = jnp.where(c < nchunks - 1, xn_ref[1:2, :], 0.0)
    xm1 = jnp.where(rows == 0, prev, pltpu.roll(x, 1, 0))
    xp1 = jnp.where(rows == tc - 1, nxt0, pltpu.roll(x, tc - 1, 0))
    xp2 = jnp.where(rows == tc - 2, nxt0, jnp.where(rows == tc - 1, nxt1, pltpu.roll(x, tc - 2, 0)))
    cw = cw_ref[...]
    xc = cb_ref[...] + xm1 * cw[0:1] + x * cw[1:2] + xp1 * cw[2:3] + xp2 * cw[3:4]

    xcb = xc.astype(BF16)
    r_parts, i_parts = [], []
    for n in range(LRU_BLOCKS):
        xs = xcb[:, n * LRU_BW:(n + 1) * LRU_BW]
        r_parts.append(jnp.dot(xs, wa_ref[n], preferred_element_type=F32))
        i_parts.append(jnp.dot(xs, wx_ref[n], preferred_element_type=F32))
    r = jax.nn.sigmoid(jnp.concatenate(r_parts, axis=1) + ba_ref[...])
    ig = jax.nn.sigmoid(jnp.concatenate(i_parts, axis=1) + bx_ref[...])
    z = -lam_ref[...]
    softplus = jnp.maximum(z, 0.0) + jnp.log1p(jnp.exp(-jnp.abs(z)))
    log_a = -LRU_C * r * softplus
    a = jnp.exp(log_a)
    mult = jnp.sqrt(-jnp.tanh(log_a) * (a * a + 1.0))
    edge_row = tc - 1 if reverse else 0
    edge_chunk = nchunks - 1 if reverse else 0
    mult = jnp.where(rows == jnp.where(c == edge_chunk, edge_row, -1), 1.0, mult)
    a_s[...] = a
    b_s[...] = mult * ig * xc

    row8 = lax.broadcasted_iota(jnp.int32, (SUBLANES, D_RNN), 0)
    ngroups = tc // SUBLANES
    h0 = jnp.where(i == 0, 0.0, carry[...])

    def step(gi, h):
        g = (ngroups - 1 - gi) if reverse else gi
        off = pl.multiple_of(g * SUBLANES, SUBLANES)
        av = a_s[pl.ds(off, SUBLANES), :]
        bv = b_s[pl.ds(off, SUBLANES), :]
        for s in (1, 2, 4):
            if reverse:
                keep = row8 < SUBLANES - s
                shift = SUBLANES - s
            else:
                keep = row8 >= s
                shift = s
            a_sh = jnp.where(keep, pltpu.roll(av, shift, 0), 1.0)
            b_sh = jnp.where(keep, pltpu.roll(bv, shift, 0), 0.0)
            bv = av * b_sh + bv
            av = av * a_sh
        hv = av * h + bv
        h_ref[pl.ds(off, SUBLANES), :] = hv
        last = hv[0:1, :] if reverse else hv[SUBLANES - 1:SUBLANES, :]
        return jnp.broadcast_to(last, (SUBLANES, D_RNN))

    carry[...] = lax.fori_loop(0, ngroups, step, h0)


def _lru(proj, row0, nb, s, cw, cb, wa, ba, wx, bx, lam, reverse):
    tc = 512
    nchunks = s // tc
    t_all = proj.shape[0]
    blk0 = row0 // tc
    last8 = t_all // SUBLANES - 1

    def cidx(i):
        return (nchunks - 1 - i) if reverse else i

    def cur(b, i):
        return (blk0 + b * nchunks + cidx(i), COL_XRNN)

    def prev8(b, i):
        r = (blk0 + b * nchunks + cidx(i)) * (tc // SUBLANES) - 1
        return (jnp.maximum(r, 0), COL_XRNN)

    def next8(b, i):
        r = (blk0 + b * nchunks + cidx(i) + 1) * (tc // SUBLANES)
        return (jnp.minimum(r, last8), COL_XRNN)

    full = lambda shape: pl.BlockSpec(shape, lambda b, i: (0,) * len(shape))
    body = functools.partial(_lru_body, reverse=reverse, nchunks=nchunks, tc=tc)
    return pl.pallas_call(
        body,
        grid=(nb, nchunks),
        in_specs=[
            pl.BlockSpec((tc, D_RNN), cur),
            pl.BlockSpec((SUBLANES, D_RNN), prev8),
            pl.BlockSpec((SUBLANES, D_RNN), next8),
            full((CONV_W, D_RNN)), full((1, D_RNN)),
            full((LRU_BLOCKS, LRU_BW, LRU_BW)), full((1, D_RNN)),
            full((LRU_BLOCKS, LRU_BW, LRU_BW)), full((1, D_RNN)),
            full((1, D_RNN)),
        ],
        out_specs=pl.BlockSpec((tc, D_RNN), lambda b, i: (b * nchunks + cidx(i), 0)),
        out_shape=jax.ShapeDtypeStruct((nb * s, D_RNN), F32),
        scratch_shapes=[pltpu.VMEM((tc, D_RNN), F32), pltpu.VMEM((tc, D_RNN), F32),
                        pltpu.VMEM((SUBLANES, D_RNN), F32)],
        compiler_params=pltpu.CompilerParams(dimension_semantics=("arbitrary", "arbitrary")),
        name="lru_bwd" if reverse else "lru_fwd",
    )(proj, proj, proj, cw, cb, wa, ba, wx, bx, lam)


def _attn_body(q_ref, k_ref, vt_ref, bt_ref, lamv_ref, g_ref, o_ref,
               q1_s, q2_s, m1_s, acc1_s, m2_s, acc2_s, sa_s, sb_s, *, nkv, tkv):
    i = pl.program_id(2)

    q = q_ref[...].astype(F32) * (HEAD_DIM ** -0.5 * LOG2E)
    lane = lax.broadcasted_iota(jnp.int32, q.shape, 1)
    q1_s[...] = jnp.where(lane < HEAD_DIM, q, 0.0).astype(BF16)
    q2_s[...] = jnp.where(lane >= HEAD_DIM, q, 0.0).astype(BF16)
    for m_s, acc_s in ((m1_s, acc1_s), (m2_s, acc2_s)):
        m_s[...] = jnp.full(m_s.shape, -jnp.inf, F32)
        acc_s[...] = jnp.zeros(acc_s.shape, F32)

    def scores(j, buf):
        off = pl.multiple_of(j * tkv, tkv)
        k = k_ref[pl.ds(off, tkv), :]
        bias = bt_ref[jnp.clip(j - i, -BIAS_REACH, BIAS_REACH) + BIAS_REACH]
        for c, q_s in enumerate((q1_s, q2_s)):
            buf[c] = lax.dot_general(k, q_s[...], NT_DIMS, preferred_element_type=F32) + bias

    def accumulate(j, buf):
        off = pl.multiple_of(j * tkv, tkv)
        vt = vt_ref[:, pl.ds(off, tkv)]
        for c, (m_s, acc_s) in enumerate(((m1_s, acc1_s), (m2_s, acc2_s))):
            s = buf[c]
            m_old = m_s[...]
            m_new = jnp.maximum(m_old, jnp.max(s, axis=0, keepdims=True))
            alpha = jnp.exp2(m_old - m_new)
            p = jnp.exp2((s - m_new).astype(BF16))
            acc_s[...] = alpha * acc_s[...] + jnp.dot(vt, p, preferred_element_type=F32)
            m_s[...] = m_new

    scores(0, sa_s)

    def pair(u, carry):
        j = 2 * u
        scores(j + 1, sb_s)
        accumulate(j, sa_s)
        scores(j + 2, sa_s)
        accumulate(j + 1, sb_s)
        return carry

    lax.fori_loop(0, nkv // 2 - 1, pair, 0)
    scores(nkv - 1, sb_s)
    accumulate(nkv - 2, sa_s)
    accumulate(nkv - 1, sb_s)

    lv = lamv_ref[...]
    lam = (jnp.exp(jnp.sum(lv[0:1] * lv[1:2], axis=-1, keepdims=True))
           - jnp.exp(jnp.sum(lv[2:3] * lv[3:4], axis=-1, keepdims=True)) + LAM_INIT)
    o = (acc1_s[0:V_DIM, :] / acc1_s[V_DIM:V_DIM + 1, :]
         - lam * (acc2_s[0:V_DIM, :] / acc2_s[V_DIM:V_DIM + 1, :]))
    o_ref[...] = _rms(o.T, g_ref[...]) * (1.0 - LAM_INIT)


def _attention(qk, vt, row0, nb, s, btiles, lamv, subln_g):
    tq = btiles.shape[-1]
    nq = s // tq
    assert nq >= 2 and nq % 2 == 0
    blk0 = row0 // tq
    seq0 = row0 // s

    body = functools.partial(_attn_body, nkv=nq, tkv=tq)
    stat = pltpu.VMEM((1, tq), F32)
    acc = pltpu.VMEM((V_AUG, tq), F32)
    sbuf = pltpu.VMEM((2, tq, tq), F32)
    return pl.pallas_call(
        body,
        grid=(nb, N_HEADS, nq),
        in_specs=[
            pl.BlockSpec((None, None, tq, V_DIM), lambda b, h, i: (0, h, blk0 + b * nq + i, 0)),
            pl.BlockSpec((None, None, s, V_DIM), lambda b, h, i: (1, h, seq0 + b, 0)),
            pl.BlockSpec((None, V_AUG, s), lambda b, h, i: (h, 0, seq0 + b)),
            pl.BlockSpec((None, 2 * BIAS_REACH + 1, tq, tq), lambda b, h, i: (h, 0, 0, 0)),
            pl.BlockSpec((4, HEAD_DIM), lambda b, h, i: (0, 0)),
            pl.BlockSpec((1, V_DIM), lambda b, h, i: (0, 0)),
        ],
        out_specs=pl.BlockSpec((tq, V_DIM), lambda b, h, i: (b * nq + i, h)),
        out_shape=jax.ShapeDtypeStruct((nb * s, ATTN_W), F32),
        scratch_shapes=[pltpu.VMEM((tq, V_DIM), BF16), pltpu.VMEM((tq, V_DIM), BF16),
                        stat, acc, stat, acc, sbuf, sbuf],
        compiler_params=pltpu.CompilerParams(
            dimension_semantics=("parallel", "parallel", "arbitrary"),
            vmem_limit_bytes=VMEM_LIMIT),
        name="diff_attn",
    )(qk, qk, vt, btiles, lamv, subln_g)


def _merge_body(xg0_ref, xg1_ref, hf0_ref, hf1_ref, hb0_ref, hb1_ref, at0_ref, at1_ref, grnn_ref, glr_ref, gla_ref, bg_ref,
                wr_ref, wa_ref, wo_ref, n2_ref, wq_ref, x1_ref, xn2_ref, qp_ref, *, nblk0):
    first = pl.program_id(0) < nblk0
    x = jnp.where(first, xg0_ref[...], xg1_ref[...])
    hf = jnp.where(first, hf0_ref[...], hf1_ref[...])
    hb = jnp.where(first, hb0_ref[...], hb1_ref[...])
    attn = jnp.where(first, at0_ref[...], at1_ref[...])
    hg = ((hf + hb) * jax.nn.gelu(grnn_ref[...])).astype(BF16)
    y_rnn = jnp.dot(hg, wr_ref[...], preferred_element_type=F32)
    y_attn = jnp.dot(attn.astype(BF16), wa_ref[...], preferred_element_type=F32)
    bg = bg_ref[...]
    g_r = jax.nn.sigmoid(glr_ref[...] + bg[0:1])
    g_a = jax.nn.sigmoid(gla_ref[...] + bg[1:2])
    merged = (g_r * y_rnn + g_a * y_attn).astype(BF16)
    x1 = x + jnp.dot(merged, wo_ref[...], preferred_element_type=F32)
    x1_ref[...] = x1
    xn2 = _rms(x1, n2_ref[...])
    xn2_ref[...] = xn2
    qp_ref[...] = jnp.dot(xn2.astype(BF16), wq_ref[...], preferred_element_type=F32)


def _merge(xs, hf, hb, proj, attn, bg, wr, wa, wo, n2, wq):
    tm = 256
    nblk0 = hf[0].shape[0] // tm
    nblk1 = hf[1].shape[0] // tm
    t = (nblk0 + nblk1) * tm
    row = lambda c: pl.BlockSpec((tm, D_MODEL), lambda i: (i, c))
    rows0, rows1 = _group_rows(tm, nblk0, nblk1)
    grp0 = pl.BlockSpec((tm, D_MODEL), rows0)
    grp1 = pl.BlockSpec((tm, D_MODEL), rows1)
    full = lambda shape: pl.BlockSpec(shape, lambda i: (0,) * len(shape))
    qw = PEER_HEADS * D_KEY
    return pl.pallas_call(
        functools.partial(_merge_body, nblk0=nblk0),
        grid=(t // tm,),
        in_specs=[grp0, grp1, grp0, grp1, grp0, grp1, grp0, grp1, row(COL_GRNN), row(COL_GLR), row(COL_GLA),
                  full((2, D_MODEL)), full((D_RNN, D_MODEL)), full((ATTN_W, D_MODEL)),
                  full((D_MODEL, D_MODEL)), full((1, D_MODEL)), full((D_MODEL, qw))],
        out_specs=[row(0), row(0), pl.BlockSpec((tm, qw), lambda i: (i, 0))],
        out_shape=[jax.ShapeDtypeStruct((t, D_MODEL), F32), jax.ShapeDtypeStruct((t, D_MODEL), F32),
                   jax.ShapeDtypeStruct((t, qw), F32)],
        compiler_params=pltpu.CompilerParams(dimension_semantics=("parallel",),
                                             vmem_limit_bytes=VMEM_LIMIT),
        name="merge_proj",
    )(xs[0], xs[1], hf[0], hf[1], hb[0], hb[1], attn[0], attn[1], proj, proj, proj, bg, wr, wa, wo, n2, wq)


def _topk_rows(s, k, ids=None):
    if ids is None:
        ids = lax.broadcasted_iota(jnp.int32, s.shape, 0).astype(F32)
    slot = lax.broadcasted_iota(jnp.int32, (k, s.shape[1]), 0)
    vals = jnp.zeros((k, s.shape[1]), F32)
    idxs = jnp.zeros((k, s.shape[1]), F32)
    for t in range(k):
        m = jnp.max(s, axis=0, keepdims=True)
        pos = jnp.min(jnp.where(s == m, ids, jnp.inf), axis=0, keepdims=True)
        vals = jnp.where(slot == t, m, vals)
        idxs = jnp.where(slot == t, pos, idxs)
        s = jnp.where(ids == pos, -jnp.inf, s)
    return vals, idxs


_PAIRS = [(a, b) for a in range(PEER_TOPK) for b in range(PEER_TOPK) if (a + 1) * (b + 1) <= PEER_TOPK]


def _rows_from(src, picks, fill):
    if all(p is not None for p in picks) and picks[0] % SUBLANES == 0 and picks == list(range(picks[0], picks[0] + SUBLANES)):
        return src[picks[0]:picks[0] + SUBLANES, :]
    row = lax.broadcasted_iota(jnp.int32, (SUBLANES, src.shape[1]), 0)
    out = jnp.full((SUBLANES, src.shape[1]), fill, F32)
    for r, p in enumerate(picks):
        if p is not None:
            out = jnp.where(row == r, src[p:p + 1, :], out)
    return out


def _pair_candidates(sv1, sv2):
    pairs = _PAIRS + [None] * (-len(_PAIRS) % SUBLANES)
    width = sv1.shape[1]
    row = lax.broadcasted_iota(jnp.int32, (SUBLANES, width), 0)
    sums, ids = [], []
    for g0 in range(0, len(pairs), SUBLANES):
        grp = pairs[g0:g0 + SUBLANES]
        a_rows = _rows_from(sv1, [None if pr is None else pr[0] for pr in grp], -jnp.inf)
        b_rows = _rows_from(sv2, [None if pr is None else pr[1] for pr in grp], 0.0)
        sums.append(a_rows + b_rows)
        idv = jnp.full((SUBLANES, width), float(PEER_TOPK * PEER_TOPK), F32)
        for r, pr in enumerate(grp):
            if pr is not None:
                idv = jnp.where(row == r, float(pr[0] * PEER_TOPK + pr[1]), idv)
        ids.append(idv)
    return jnp.concatenate(sums, axis=0), jnp.concatenate(ids, axis=0)


def _router_body(qp_ref, keys_ref, idx_ref, g_ref):
    slot = lax.broadcasted_iota(jnp.int32, (PEER_TOPK, qp_ref.shape[0]), 0)
    for h in range(PEER_HEADS):
        sv, si = [], []
        for p in range(2):
            c0 = (h * 2 + p) * D_HALF
            q = qp_ref[:, c0:c0 + D_HALF].astype(BF16)
            s = lax.dot_general(keys_ref[h, p], q, (((1,), (1,)), ((), ())), preferred_element_type=F32)
            v, ix = _topk_rows(s, PEER_TOPK)
            sv.append(v)
            si.append(ix)
        cand, cand_id = _pair_candidates(sv[0], sv[1])
        top_s, pos = _topk_rows(cand, PEER_TOPK, cand_id)
        pos = pos.astype(jnp.int32)
        pa = pos >> 4
        pb = pos & (PEER_TOPK - 1)
        idx = jnp.zeros(slot.shape, F32)
        for t in range(PEER_TOPK):
            i1 = jnp.sum(jnp.where(slot == pa[t:t + 1, :], si[0], 0.0), axis=0, keepdims=True)
            i2 = jnp.sum(jnp.where(slot == pb[t:t + 1, :], si[1], 0.0), axis=0, keepdims=True)
            idx = jnp.where(slot == t, (i1 * N_KEYS + i2) * ROW_SUB, idx)
        idx = idx.astype(jnp.int32)
        e = jnp.exp(top_s - jnp.max(top_s, axis=0, keepdims=True))
        g = e / jnp.sum(e, axis=0, keepdims=True)
        idx_ref[h * PEER_TOPK:(h + 1) * PEER_TOPK, :] = idx
        g_ref[h * PEER_TOPK:(h + 1) * PEER_TOPK, :] = g


def _router(qp, keys):
    t = qp.shape[0]
    c = 256
    qw = PEER_HEADS * D_KEY
    return pl.pallas_call(
        _router_body,
        grid=(t // c,),
        in_specs=[pl.BlockSpec((c, qw), lambda i: (i, 0)),
                  pl.BlockSpec((PEER_HEADS, 2, N_KEYS, D_HALF), lambda i: (0, 0, 0, 0))],
        out_specs=[pl.BlockSpec((N_SEL, c), lambda i: (0, i)), pl.BlockSpec((N_SEL, c), lambda i: (0, i))],
        out_shape=[jax.ShapeDtypeStruct((N_SEL, t), jnp.int32), jax.ShapeDtypeStruct((N_SEL, t), F32)],
        compiler_params=pltpu.CompilerParams(dimension_semantics=("parallel",)),
        name="peer_router",
    )(qp, keys)


def _pack_body(t_ref, o_ref):
    o_ref[...] = pltpu.bitcast(t_ref[...].astype(BF16), jnp.int32)


def _pack_table(tab2):
    n8 = tab2.shape[0]
    r = 4096
    return pl.pallas_call(
        _pack_body,
        grid=(n8 // r,),
        in_specs=[pl.BlockSpec((r, LANES), lambda i: (i, 0))],
        out_specs=pl.BlockSpec((r // 2, LANES), lambda i: (i, 0)),
        out_shape=jax.ShapeDtypeStruct((n8 // 2, LANES), jnp.int32),
        compiler_params=pltpu.CompilerParams(dimension_semantics=("parallel",)),
        name="peer_pack",
    )(tab2)


PEER_UNROLL = SUBLANES
PEER_U_BLOCK = 128
PEER_V_BLOCK = 256


def _load_table(tab_hbm, tab_s, sem):
    @pl.when(pl.program_id(0) == 0)
    def _():
        cp = pltpu.make_async_copy(tab_hbm, tab_s, sem)
        cp.start()
        cp.wait()


def _gather_rows(tab_s, idx_ref, t, m_ref):
    for m in range(N_SEL // 2):
        word = idx_ref[t, m]
        for half, r in enumerate((word & 0xFFFF, lax.shift_right_logical(word, 16))):
            j = 2 * m + half
            r = pl.multiple_of(r, ROW_SUB)
            m_ref[j * ROW_SUB:(j + 1) * ROW_SUB, :] = tab_s[pl.ds(r, ROW_SUB), :]


def _split3_bf16(x):
    hi = x.astype(BF16).astype(F32)
    r1 = x - hi
    mid = r1.astype(BF16).astype(F32)
    lo = r1 - mid
    return jnp.concatenate([hi, mid, lo, jnp.zeros_like(x)], axis=0).astype(BF16)


def _sum3(y):
    return y[0:SUBLANES] + y[SUBLANES:2 * SUBLANES] + y[2 * SUBLANES:3 * SUBLANES]


def _diag_mask():
    p = lax.broadcasted_iota(jnp.int32, (SUBLANES, N_SEL * SUBLANES), 0)
    c = lax.broadcasted_iota(jnp.int32, (SUBLANES, N_SEL * SUBLANES), 1)
    return (c & (SUBLANES - 1)) == p


def _pipelined_groups(ngroups, gather, compute):
    assert ngroups >= 2 and ngroups % 2 == 0
    gather(0, 0)

    def pair(p, carry):
        g = 2 * p
        gather(g + 1, 1)
        compute(g, 0)
        gather(g + 2, 0)
        compute(g + 1, 1)
        return carry

    lax.fori_loop(0, ngroups // 2 - 1, pair, 0)
    gather(ngroups - 1, 1)
    compute(ngroups - 2, 0)
    compute(ngroups - 1, 1)


def _peer_u_body(idx_ref, x_ref, g_ref, tab_hbm, w_ref, tab_s, m_s, fold_s, p_s, s_s, sem, *, tb):
    _load_table(tab_hbm, tab_s, sem)

    @pl.when(pl.program_id(0) == 0)
    def _():
        r = lax.broadcasted_iota(jnp.int32, fold_s.shape, 0)
        c = lax.broadcasted_iota(jnp.int32, fold_s.shape, 1)
        fold_s[...] = jnp.where((r >> 3) == c, 1.0, 0.0).astype(BF16)

    diag = _diag_mask()

    def gather(gi, half):
        for u in range(PEER_UNROLL):
            _gather_rows(tab_s, idx_ref, gi * PEER_UNROLL + u, m_s.at[half * PEER_UNROLL + u])

    def compute(gi, half):
        for u in range(PEER_UNROLL):
            t = gi * PEER_UNROLL + u
            rows = pltpu.bitcast(m_s[half * PEER_UNROLL + u], BF16)
            p = _sum3(lax.dot_general(_split3_bf16(x_ref[t]), rows, NT_DIMS, preferred_element_type=F32))
            p_s[t] = jnp.where(diag, p, 0.0)

    _pipelined_groups(tb // PEER_UNROLL, gather, compute)

    pm = p_s[...].reshape(tb * SUBLANES, N_SEL * SUBLANES)
    hi = pm.astype(BF16)
    lo = (pm - hi.astype(F32)).astype(BF16)
    fold = fold_s[...]
    y = jnp.dot(hi, fold, preferred_element_type=F32) + jnp.dot(lo, fold, preferred_element_type=F32)
    s_s[...] = y.reshape(tb, SUBLANES, N_SEL)
    s = s_s[:, 0, :]
    for q in range(1, SUBLANES):
        s = s + s_s[:, q, :]
    w_ref[...] = g_ref[...] * jax.nn.gelu(s)


def _peer_u(idx_t, x3, g_t, tab):
    t = idx_t.shape[0]
    tb = PEER_U_BLOCK
    body = functools.partial(_peer_u_body, tb=tb)
    return pl.pallas_call(
        body,
        grid=(t // tb,),
        in_specs=[pl.BlockSpec((tb, N_SEL // 2), lambda i: (i, 0), memory_space=pltpu.SMEM),
                  pl.BlockSpec((tb, SUBLANES, LANES), lambda i: (i, 0, 0)),
                  pl.BlockSpec((tb, N_SEL), lambda i: (i, 0)),
                  pl.BlockSpec(memory_space=pl.ANY)],
        out_specs=pl.BlockSpec((tb, N_SEL), lambda i: (i, 0)),
        out_shape=jax.ShapeDtypeStruct((t, N_SEL), F32),
        scratch_shapes=[pltpu.VMEM((N_EXPERTS * ROW_SUB, LANES), jnp.int32),
                        pltpu.VMEM((2 * PEER_UNROLL, N_SEL * ROW_SUB, LANES), jnp.int32),
                        pltpu.VMEM((N_SEL * SUBLANES, N_SEL), BF16),
                        pltpu.VMEM((tb, SUBLANES, N_SEL * SUBLANES), F32),
                        pltpu.VMEM((tb, SUBLANES, LANES), F32),
                        pltpu.SemaphoreType.DMA(())],
        compiler_params=pltpu.CompilerParams(dimension_semantics=("arbitrary",),
                                             vmem_limit_bytes=VMEM_LIMIT),
        name="peer_u",
    )(idx_t, x3, g_t, tab)


def _peer_v_body(idx_ref, w_ref, tab_hbm, o_ref, tab_s, m_s, spread_s, sem, *, tb):
    _load_table(tab_hbm, tab_s, sem)

    @pl.when(pl.program_id(0) == 0)
    def _():
        r = lax.broadcasted_iota(jnp.int32, spread_s.shape, 0)
        c = lax.broadcasted_iota(jnp.int32, spread_s.shape, 1)
        spread_s[...] = jnp.where((c >> 3) == r, 1.0, 0.0).astype(BF16)

    diag = _diag_mask()
    zero = jnp.zeros((SUBLANES, N_SEL * SUBLANES), F32)

    def gather(gi, half):
        for u in range(PEER_UNROLL):
            _gather_rows(tab_s, idx_ref, gi * PEER_UNROLL + u, m_s.at[half * PEER_UNROLL + u])

    def compute(gi, half):
        base = pl.multiple_of(gi * PEER_UNROLL, PEER_UNROLL)
        e = jnp.dot(_split3_bf16(w_ref[pl.ds(base, PEER_UNROLL), :]), spread_s[...], preferred_element_type=F32)
        for u in range(PEER_UNROLL):
            rows = pltpu.bitcast(m_s[half * PEER_UNROLL + u], BF16)
            terms = [jnp.where(diag, jnp.broadcast_to(e[SUBLANES * k + u:SUBLANES * k + u + 1, :], diag.shape), 0.0)
                     for k in range(3)]
            lhs = jnp.concatenate(terms + [zero], axis=0).astype(BF16)
            o_ref[base + u] = _sum3(jnp.dot(lhs, rows, preferred_element_type=F32))

    _pipelined_groups(tb // PEER_UNROLL, gather, compute)


def _peer_v(idx_t, w, tab):
    t = idx_t.shape[0]
    tb = PEER_V_BLOCK
    body = functools.partial(_peer_v_body, tb=tb)
    return pl.pallas_call(
        body,
        grid=(t // tb,),
        in_specs=[pl.BlockSpec((tb, N_SEL // 2), lambda i: (i, 0), memory_space=pltpu.SMEM),
                  pl.BlockSpec((tb, N_SEL), lambda i: (i, 0)),
                  pl.BlockSpec(memory_space=pl.ANY)],
        out_specs=pl.BlockSpec((tb, SUBLANES, LANES), lambda i: (i, 0, 0)),
        out_shape=jax.ShapeDtypeStruct((t, SUBLANES, LANES), F32),
        scratch_shapes=[pltpu.VMEM((N_EXPERTS * ROW_SUB, LANES), jnp.int32),
                        pltpu.VMEM((2 * PEER_UNROLL, N_SEL * ROW_SUB, LANES), jnp.int32),
                        pltpu.VMEM((N_SEL, N_SEL * SUBLANES), BF16),
                        pltpu.SemaphoreType.DMA(())],
        compiler_params=pltpu.CompilerParams(dimension_semantics=("arbitrary",),
                                             vmem_limit_bytes=VMEM_LIMIT),
        name="peer_v",
    )(idx_t, w, tab)


def _final_body(x1_ref, o_ref, g_ref, y_ref):
    y_ref[...] = _rms(x1_ref[...] + o_ref[...], g_ref[...])


def _final(x1, o, g):
    t = x1.shape[0]
    tm = 512
    row = pl.BlockSpec((tm, D_MODEL), lambda i: (i, 0))
    return pl.pallas_call(
        _final_body,
        grid=(t // tm,),
        in_specs=[row, row, pl.BlockSpec((1, D_MODEL), lambda i: (0, 0))],
        out_specs=row,
        out_shape=jax.ShapeDtypeStruct((t, D_MODEL), F32),
        compiler_params=pltpu.CompilerParams(dimension_semantics=("parallel",)),
        name="final_norm",
    )(x1, o, g)


def _rel_bucket(rel):
    nb = NUM_BUCKETS // 2
    ret = jnp.where(rel > 0, nb, 0).astype(jnp.int32)
    n = jnp.abs(rel)
    max_exact = nb // 2
    nf = jnp.maximum(n, 1).astype(F32)
    large = max_exact + (jnp.log(nf / max_exact) / math.log(MAX_DISTANCE / max_exact) * (nb - max_exact)).astype(jnp.int32)
    large = jnp.minimum(large, nb - 1)
    return ret + jnp.where(n < max_exact, n, large)


def _bias_tables(rel_bias, tq):
    assert tq >= MAX_DISTANCE
    m = jnp.arange(2 * tq - 1, dtype=jnp.int32)
    rel = jnp.stack([d * tq + m - (tq - 1) for d in range(-BIAS_REACH, BIAS_REACH + 1)])
    w = jnp.transpose(rel_bias[_rel_bucket(rel)], (2, 0, 1)).astype(F32) * LOG2E
    period = 2 * tq
    v = jnp.concatenate([w[..., tq - 1::-1], jnp.zeros(w.shape[:2] + (1,), F32), w[..., :tq - 1:-1]], axis=-1)
    rep = jnp.tile(v, (1, 1, tq))[..., :tq * (period - 1)]
    return rep.reshape(w.shape[:2] + (tq, period - 1))[..., :tq]


def kernel(x_prompt, x_sample, norm1_g, w_in, b_gate, conv_w, conv_b, lru_wa_f, lru_ba_f, lru_wx_f, lru_bx_f, lru_lam_f, lru_wa_b, lru_ba_b, lru_wx_b, lru_bx_b, lru_lam_b, lam_q1, lam_k1, lam_q2, lam_k2, subln_g, rel_bias, w_rnn_out, w_attn_out, w_out, norm2_g, peer_wq, peer_keys, peer_u, peer_v, final_g):
    groups = [x_prompt.shape[:2], x_sample.shape[:2]]
    xs = (x_prompt.reshape(-1, D_MODEL), x_sample.reshape(-1, D_MODEL))
    t = xs[0].shape[0] + xs[1].shape[0]
    l = 0
    row = lambda a: a.reshape(1, -1)

    w_main = jnp.concatenate([w_in[l][:, :QKV_COL0], w_in[l][:, QKV_COL1:]], axis=1).astype(BF16)
    proj = _inproj(xs, row(norm1_g[l]), w_main)
    qk, vt = _qkv(xs, row(norm1_g[l]), w_in[l][:, QKV_COL0:QKV_COL1].astype(BF16))

    btiles = _bias_tables(rel_bias, 512)
    lamv = jnp.stack([lam_q1[l], lam_k1[l], lam_q2[l], lam_k2[l]])
    lru_f = (conv_w[l], row(conv_b[l]), lru_wa_f[l].astype(BF16), row(lru_ba_f[l]),
             lru_wx_f[l].astype(BF16), row(lru_bx_f[l]), row(lru_lam_f[l]))
    lru_b = (conv_w[l], row(conv_b[l]), lru_wa_b[l].astype(BF16), row(lru_ba_b[l]),
             lru_wx_b[l].astype(BF16), row(lru_bx_b[l]), row(lru_lam_b[l]))
    hf, hb, attn = [], [], []
    row0 = 0
    for nb, s in groups:
        hf.append(_lru(proj, row0, nb, s, *lru_f, reverse=False))
        hb.append(_lru(proj, row0, nb, s, *lru_b, reverse=True))
        attn.append(_attention(qk, vt, row0, nb, s, btiles, lamv, row(subln_g[l])))
        row0 += nb * s

    x1, xn2, qp = _merge(xs, hf, hb, proj, attn, b_gate[l].reshape(2, D_MODEL),
                         w_rnn_out[l].astype(BF16), w_attn_out[l].astype(BF16), w_out[l].astype(BF16),
                         row(norm2_g[l]), peer_wq[l].astype(BF16))

    idx, g = _router(qp, peer_keys[l].astype(BF16))
    idx_t = idx.T
    idx_t = idx_t[:, 0::2] | (idx_t[:, 1::2] << 16)
    g_t = g.T
    tab_u = _pack_table(peer_u[l].reshape(N_EXPERTS * SUBLANES, LANES))
    tab_v = _pack_table(peer_v[l].reshape(N_EXPERTS * SUBLANES, LANES))
    w = _peer_u(idx_t, xn2.reshape(t, SUBLANES, LANES), g_t, tab_u)
    o = _peer_v(idx_t, w, tab_v).reshape(t, D_MODEL)

    y = _final(x1, o, row(final_g))
    n0 = groups[0][0] * groups[0][1]
    return (y[:n0].reshape(x_prompt.shape), y[n0:].reshape(x_sample.shape))
```

```python
import functools
import math

import jax
import jax.numpy as jnp
from jax import lax
from jax.experimental import pallas as pl
from jax.experimental.pallas import tpu as pltpu

F32 = jnp.float32
BF16 = jnp.bfloat16

D_MODEL = 1024
D_RNN = 1024
LRU_BLOCKS = 4
LRU_BW = D_RNN // LRU_BLOCKS
LRU_C = 8.0
CONV_W = 4
N_HEADS = 8
HEAD_DIM = 64
V_DIM = 2 * HEAD_DIM
V_AUG = V_DIM + 16
ATTN_W = N_HEADS * V_DIM
NUM_BUCKETS = 32
MAX_DISTANCE = 128
PEER_HEADS = 8
N_KEYS = 128
N_EXPERTS = N_KEYS * N_KEYS
PEER_TOPK = 16
D_KEY = 256
D_HALF = D_KEY // 2
N_SEL = PEER_HEADS * PEER_TOPK
IN_W = 2 * D_RNN + 3 * ATTN_W + 2 * D_MODEL
EPS = 1e-6
LAM_INIT = 0.8 - 0.6 * math.exp(-0.3 * 0)
LOG2E = math.log2(math.e)
NT_DIMS = (((1,), (1,)), ((), ()))
BIAS_REACH = 2

COL_XRNN, COL_GRNN, COL_GLR, COL_GLA = range(4)
QKV_COL0 = 2 * D_RNN
QKV_COL1 = QKV_COL0 + 3 * ATTN_W

SUBLANES = 8
LANES = 128
ROW_WORDS = D_MODEL // 2
ROW_SUB = ROW_WORDS // LANES
VMEM_LIMIT = 56 * 1024 * 1024


def _rms(x, g):
    return x * lax.rsqrt(jnp.mean(x * x, axis=-1, keepdims=True) + EPS) * g


def _group_rows(tm, nblk0, nblk1):
    return (lambda i, *_: (jnp.minimum(i, nblk0 - 1), 0),
            lambda i, *_: (jnp.clip(i - nblk0, 0, nblk1 - 1), 0))


def _inproj_body(x0_ref, x1_ref, g_ref, w_ref, o_ref, xn_ref, *, nblk0):
    @pl.when(pl.program_id(1) == 0)
    def _():
        x = jnp.where(pl.program_id(0) < nblk0, x0_ref[...], x1_ref[...])
        xn_ref[...] = _rms(x, g_ref[...]).astype(BF16)

    o_ref[...] = jnp.dot(xn_ref[...], w_ref[...], preferred_element_type=F32)


def _inproj(xs, g, w):
    n_out = w.shape[1]
    tm, tn = 1024, 1024
    nblk0, nblk1 = xs[0].shape[0] // tm, xs[1].shape[0] // tm
    rows0, rows1 = _group_rows(tm, nblk0, nblk1)
    return pl.pallas_call(
        functools.partial(_inproj_body, nblk0=nblk0),
        grid=(nblk0 + nblk1, n_out // tn),
        in_specs=[
            pl.BlockSpec((tm, D_MODEL), rows0),
            pl.BlockSpec((tm, D_MODEL), rows1),
            pl.BlockSpec((1, D_MODEL), lambda i, j: (0, 0)),
            pl.BlockSpec((D_MODEL, tn), lambda i, j: (0, j)),
        ],
        out_specs=pl.BlockSpec((tm, tn), lambda i, j: (i, j)),
        out_shape=jax.ShapeDtypeStruct(((nblk0 + nblk1) * tm, n_out), F32),
        scratch_shapes=[pltpu.VMEM((tm, D_MODEL), BF16)],
        compiler_params=pltpu.CompilerParams(dimension_semantics=("parallel", "arbitrary")),
        name="inproj",
    )(xs[0], xs[1], g, w)


def _qkv_body(x0_ref, x1_ref, g_ref, w_ref, qk_ref, vt_ref, xn_ref, *, nblk0):
    j = pl.program_id(1)

    @pl.when(j == 0)
    def _():
        x = jnp.where(pl.program_id(0) < nblk0, x0_ref[...], x1_ref[...])
        xn_ref[...] = _rms(x, g_ref[...]).astype(BF16)

    res = jnp.dot(xn_ref[...], w_ref[...], preferred_element_type=F32)

    @pl.when(j < 2)
    def _():
        for h in range(N_HEADS):
            qk_ref[h] = res[:, h * V_DIM:(h + 1) * V_DIM].astype(BF16)

    @pl.when(j == 2)
    def _():
        for h in range(N_HEADS):
            vt_ref[h, 0:V_DIM, :] = res[:, h * V_DIM:(h + 1) * V_DIM].T.astype(BF16)
            vt_ref[h, V_DIM:V_AUG, :] = jnp.ones((V_AUG - V_DIM, res.shape[0]), BF16)


def _qkv(xs, g, w):
    tm = 1024
    nblk0, nblk1 = xs[0].shape[0] // tm, xs[1].shape[0] // tm
    t = (nblk0 + nblk1) * tm
    rows0, rows1 = _group_rows(tm, nblk0, nblk1)
    return pl.pallas_call(
        functools.partial(_qkv_body, nblk0=nblk0),
        grid=(nblk0 + nblk1, 3),
        in_specs=[
            pl.BlockSpec((tm, D_MODEL), rows0),
            pl.BlockSpec((tm, D_MODEL), rows1),
            pl.BlockSpec((1, D_MODEL), lambda i, j: (0, 0)),
            pl.BlockSpec((D_MODEL, ATTN_W), lambda i, j: (0, j)),
        ],
        out_specs=[pl.BlockSpec((None, N_HEADS, tm, V_DIM), lambda i, j: (jnp.minimum(j, 1), 0, i, 0)),
                   pl.BlockSpec((N_HEADS, V_AUG, tm), lambda i, j: (0, 0, i))],
        out_shape=[jax.ShapeDtypeStruct((2, N_HEADS, t, V_DIM), BF16),
                   jax.ShapeDtypeStruct((N_HEADS, V_AUG, t), BF16)],
        scratch_shapes=[pltpu.VMEM((tm, D_MODEL), BF16)],
        compiler_params=pltpu.CompilerParams(dimension_semantics=("parallel", "arbitrary")),
        name="qkv_proj",
    )(xs[0], xs[1], g, w)


def _lru_body(xc_ref, xp_ref, xn_ref, cw_ref, cb_ref, wa_ref, ba_ref, wx_ref, bx_ref, lam_ref,
              h_ref, a_s, b_s, carry, *, reverse, nchunks, tc):
    i = pl.program_id(1)
    c = (nchunks - 1 - i) if reverse else i
    x = xc_ref[...]
    rows = lax.broadcasted_iota(jnp.int32, (tc, D_RNN), 0)
    prev = jnp.where(c > 0, xp_ref[SUBLANES - 1:SUBLANES, :], 0.0)
    nxt0 = jnp.where(c < nchunks - 1, xn_ref[0:1, :], 0.0)
    nxt1 = jnp.where(c < nchunks - 1, xn_ref[1:2, :], 0.0)
    xm1 = jnp.where(rows == 0, prev, pltpu.roll(x, 1, 0))
    xp1 = jnp.where(rows == tc - 1, nxt0, pltpu.roll(x, tc - 1, 0))
    xp2 = jnp.where(rows == tc - 2, nxt0, jnp.where(rows == tc - 1, nxt1, pltpu.roll(x, tc - 2, 0)))
    cw = cw_ref[...]
    xc = cb_ref[...] + xm1 * cw[0:1] + x * cw[1:2] + xp1 * cw[2:3] + xp2 * cw[3:4]

    xcb = xc.astype(BF16)
    r_parts, i_parts = [], []
    for n in range(LRU_BLOCKS):
        xs = xcb[:, n * LRU_BW:(n + 1) * LRU_BW]
        r_parts.append(jnp.dot(xs, wa_ref[n], preferred_element_type=F32))
        i_parts.append(jnp.dot(xs, wx_ref[n], preferred_element_type=F32))
    r = jax.nn.sigmoid(jnp.concatenate(r_parts, axis=1) + ba_ref[...])
    ig = jax.nn.sigmoid(jnp.concatenate(i_parts, axis=1) + bx_ref[...])
    z = -lam_ref[...]
    softplus = jnp.maximum(z, 0.0) + jnp.log1p(jnp.exp(-jnp.abs(z)))
    log_a = -LRU_C * r * softplus
    a = jnp.exp(log_a)
    mult = jnp.sqrt(-jnp.tanh(log_a) * (a * a + 1.0))
    edge_row = tc - 1 if reverse else 0
    edge_chunk = nchunks - 1 if reverse else 0
    mult = jnp.where(rows == jnp.where(c == edge_chunk, edge_row, -1), 1.0, mult)
    a_s[...] = a
    b_s[...] = mult * ig * xc

    row8 = lax.broadcasted_iota(jnp.int32, (SUBLANES, D_RNN), 0)
    ngroups = tc // SUBLANES
    h0 = jnp.where(i == 0, 0.0, carry[...])

    def step(gi, h):
        g = (ngroups - 1 - gi) if reverse else gi
        off = pl.multiple_of(g * SUBLANES, SUBLANES)
        av = a_s[pl.ds(off, SUBLANES), :]
        bv = b_s[pl.ds(off, SUBLANES), :]
        for s in (1, 2, 4):
            if reverse:
                keep = row8 < SUBLANES - s
                shift = SUBLANES - s
            else:
                keep = row8 >= s
                shift = s
            a_sh = jnp.where(keep, pltpu.roll(av, shift, 0), 1.0)
            b_sh = jnp.where(keep, pltpu.roll(bv, shift, 0), 0.0)
            bv = av * b_sh + bv
            av = av * a_sh
        hv = av * h + bv
        h_ref[pl.ds(off, SUBLANES), :] = hv
        last = hv[0:1, :] if reverse else hv[SUBLANES - 1:SUBLANES, :]
        return jnp.broadcast_to(last, (SUBLANES, D_RNN))

    carry[...] = lax.fori_loop(0, ngroups, step, h0)


def _lru(proj, row0, nb, s, cw, cb, wa, ba, wx, bx, lam, reverse):
    tc = 512
    nchunks = s // tc
    t_all = proj.shape[0]
    blk0 = row0 // tc
    last8 = t_all // SUBLANES - 1

    def cidx(i):
        return (nchunks - 1 - i) if reverse else i

    def cur(b, i):
        return (blk0 + b * nchunks + cidx(i), COL_XRNN)

    def prev8(b, i):
        r = (blk0 + b * nchunks + cidx(i)) * (tc // SUBLANES) - 1
        return (jnp.maximum(r, 0), COL_XRNN)

    def next8(b, i):
        r = (blk0 + b * nchunks + cidx(i) + 1) * (tc // SUBLANES)
        return (jnp.minimum(r, last8), COL_XRNN)

    full = lambda shape: pl.BlockSpec(shape, lambda b, i: (0,) * len(shape))
    body = functools.partial(_lru_body, reverse=reverse, nchunks=nchunks, tc=tc)
    return pl.pallas_call(
        body,
        grid=(nb, nchunks),
        in_specs=[
            pl.BlockSpec((tc, D_RNN), cur),
            pl.BlockSpec((SUBLANES, D_RNN), prev8),
            pl.BlockSpec((SUBLANES, D_RNN), next8),
            full((CONV_W, D_RNN)), full((1, D_RNN)),
            full((LRU_BLOCKS, LRU_BW, LRU_BW)), full((1, D_RNN)),
            full((LRU_BLOCKS, LRU_BW, LRU_BW)), full((1, D_RNN)),
            full((1, D_RNN)),
        ],
        out_specs=pl.BlockSpec((tc, D_RNN), lambda b, i: (b * nchunks + cidx(i), 0)),
        out_shape=jax.ShapeDtypeStruct((nb * s, D_RNN), F32),
        scratch_shapes=[pltpu.VMEM((tc, D_RNN), F32), pltpu.VMEM((tc, D_RNN), F32),
                        pltpu.VMEM((SUBLANES, D_RNN), F32)],
        compiler_params=pltpu.CompilerParams(dimension_semantics=("arbitrary", "arbitrary")),
        name="lru_bwd" if reverse else "lru_fwd",
    )(proj, proj, proj, cw, cb, wa, ba, wx, bx, lam)


def _attn_body(q_ref, k_ref, vt_ref, bt_ref, lamv_ref, g_ref, o_ref,
               q1_s, q2_s, m1_s, acc1_s, m2_s, acc2_s, sa_s, sb_s, *, nkv, tkv):
    i = pl.program_id(2)

    q = q_ref[...].astype(F32) * (HEAD_DIM ** -0.5 * LOG2E)
    lane = lax.broadcasted_iota(jnp.int32, q.shape, 1)
    q1_s[...] = jnp.where(lane < HEAD_DIM, q, 0.0).astype(BF16)
    q2_s[...] = jnp.where(lane >= HEAD_DIM, q, 0.0).astype(BF16)
    for m_s, acc_s in ((m1_s, acc1_s), (m2_s, acc2_s)):
        m_s[...] = jnp.full(m_s.shape, -jnp.inf, F32)
        acc_s[...] = jnp.zeros(acc_s.shape, F32)

    def scores(j, buf):
        off = pl.multiple_of(j * tkv, tkv)
        k = k_ref[pl.ds(off, tkv), :]
        bias = bt_ref[jnp.clip(j - i, -BIAS_REACH, BIAS_REACH) + BIAS_REACH]
        for c, q_s in enumerate((q1_s, q2_s)):
            buf[c] = lax.dot_general(k, q_s[...], NT_DIMS, preferred_element_type=F32) + bias

    def accumulate(j, buf):
        off = pl.multiple_of(j * tkv, tkv)
        vt = vt_ref[:, pl.ds(off, tkv)]
        for c, (m_s, acc_s) in enumerate(((m1_s, acc1_s), (m2_s, acc2_s))):
            s = buf[c]
            m_old = m_s[...]
            m_new = jnp.maximum(m_old, jnp.max(s, axis=0, keepdims=True))
            alpha = jnp.exp2(m_old - m_new)
            p = jnp.exp2((s - m_new).astype(BF16))
            acc_s[...] = alpha * acc_s[...] + jnp.dot(vt, p, preferred_element_type=F32)
            m_s[...] = m_new

    scores(0, sa_s)

    def pair(u, carry):
        j = 2 * u
        scores(j + 1, sb_s)
        accumulate(j, sa_s)
        scores(j + 2, sa_s)
        accumulate(j + 1, sb_s)
        return carry

    lax.fori_loop(0, nkv // 2 - 1, pair, 0)
    scores(nkv - 1, sb_s)
    accumulate(nkv - 2, sa_s)
    accumulate(nkv - 1, sb_s)

    lv = lamv_ref[...]
    lam = (jnp.exp(jnp.sum(lv[0:1] * lv[1:2], axis=-1, keepdims=True))
           - jnp.exp(jnp.sum(lv[2:3] * lv[3:4], axis=-1, keepdims=True)) + LAM_INIT)
    o = (acc1_s[0:V_DIM, :] / acc1_s[V_DIM:V_DIM + 1, :]
         - lam * (acc2_s[0:V_DIM, :] / acc2_s[V_DIM:V_DIM + 1, :]))
    o_ref[...] = _rms(o.T, g_ref[...]) * (1.0 - LAM_INIT)


def _attention(qk, vt, row0, nb, s, btiles, lamv, subln_g):
    tq = btiles.shape[-1]
    nq = s // tq
    assert nq >= 2 and nq % 2 == 0
    blk0 = row0 // tq
    seq0 = row0 // s

    body = functools.partial(_attn_body, nkv=nq, tkv=tq)
    stat = pltpu.VMEM((1, tq), F32)
    acc = pltpu.VMEM((V_AUG, tq), F32)
    sbuf = pltpu.VMEM((2, tq, tq), F32)
    return pl.pallas_call(
        body,
        grid=(nb, N_HEADS, nq),
        in_specs=[
            pl.BlockSpec((None, None, tq, V_DIM), lambda b, h, i: (0, h, blk0 + b * nq + i, 0)),
            pl.BlockSpec((None, None, s, V_DIM), lambda b, h, i: (1, h, seq0 + b, 0)),
            pl.BlockSpec((None, V_AUG, s), lambda b, h, i: (h, 0, seq0 + b)),
            pl.BlockSpec((None, 2 * BIAS_REACH + 1, tq, tq), lambda b, h, i: (h, 0, 0, 0)),
            pl.BlockSpec((4, HEAD_DIM), lambda b, h, i: (0, 0)),
            pl.BlockSpec((1, V_DIM), lambda b, h, i: (0, 0)),
        ],
        out_specs=pl.BlockSpec((tq, V_DIM), lambda b, h, i: (b * nq + i, h)),
        out_shape=jax.ShapeDtypeStruct((nb * s, ATTN_W), F32),
        scratch_shapes=[pltpu.VMEM((tq, V_DIM), BF16), pltpu.VMEM((tq, V_DIM), BF16),
                        stat, acc, stat, acc, sbuf, sbuf],
        compiler_params=pltpu.CompilerParams(
            dimension_semantics=("parallel", "parallel", "arbitrary"),
            vmem_limit_bytes=VMEM_LIMIT),
        name="diff_attn",
    )(qk, qk, vt, btiles, lamv, subln_g)


def _merge_body(xg0_ref, xg1_ref, hf0_ref, hf1_ref, hb0_ref, hb1_ref, at0_ref, at1_ref, grnn_ref, glr_ref, gla_ref, bg_ref,
                wr_ref, wa_ref, wo_ref, n2_ref, wq_ref, x1_ref, xn2_ref, qp_ref, *, nblk0):
    first = pl.program_id(0) < nblk0
    x = jnp.where(first, xg0_ref[...], xg1_ref[...])
    hf = jnp.where(first, hf0_ref[...], hf1_ref[...])
    hb = jnp.where(first, hb0_ref[...], hb1_ref[...])
    attn = jnp.where(first, at0_ref[...], at1_ref[...])
    hg = ((hf + hb) * jax.nn.gelu(grnn_ref[...])).astype(BF16)
    y_rnn = jnp.dot(hg, wr_ref[...], preferred_element_type=F32)
    y_attn = jnp.dot(attn.astype(BF16), wa_ref[...], preferred_element_type=F32)
    bg = bg_ref[...]
    g_r = jax.nn.sigmoid(glr_ref[...] + bg[0:1])
    g_a = jax.nn.sigmoid(gla_ref[...] + bg[1:2])
    merged = (g_r * y_rnn + g_a * y_attn).astype(BF16)
    x1 = x + jnp.dot(merged, wo_ref[...], preferred_element_type=F32)
    x1_ref[...] = x1
    xn2 = _rms(x1, n2_ref[...])
    xn2_ref[...] = xn2
    qp_ref[...] = jnp.dot(xn2.astype(BF16), wq_ref[...], preferred_element_type=F32)


def _merge(xs, hf, hb, proj, attn, bg, wr, wa, wo, n2, wq):
    tm = 256
    nblk0 = hf[0].shape[0] // tm
    nblk1 = hf[1].shape[0] // tm
    t = (nblk0 + nblk1) * tm
    row = lambda c: pl.BlockSpec((tm, D_MODEL), lambda i: (i, c))
    rows0, rows1 = _group_rows(tm, nblk0, nblk1)
    grp0 = pl.BlockSpec((tm, D_MODEL), rows0)
    grp1 = pl.BlockSpec((tm, D_MODEL), rows1)
    full = lambda shape: pl.BlockSpec(shape, lambda i: (0,) * len(shape))
    qw = PEER_HEADS * D_KEY
    return pl.pallas_call(
        functools.partial(_merge_body, nblk0=nblk0),
        grid=(t // tm,),
        in_specs=[grp0, grp1, grp0, grp1, grp0, grp1, grp0, grp1, row(COL_GRNN), row(COL_GLR), row(COL_GLA),
                  full((2, D_MODEL)), full((D_RNN, D_MODEL)), full((ATTN_W, D_MODEL)),
                  full((D_MODEL, D_MODEL)), full((1, D_MODEL)), full((D_MODEL, qw))],
        out_specs=[row(0), row(0), pl.BlockSpec((tm, qw), lambda i: (i, 0))],
        out_shape=[jax.ShapeDtypeStruct((t, D_MODEL), F32), jax.ShapeDtypeStruct((t, D_MODEL), F32),
                   jax.ShapeDtypeStruct((t, qw), F32)],
        compiler_params=pltpu.CompilerParams(dimension_semantics=("parallel",),
                                             vmem_limit_bytes=VMEM_LIMIT),
        name="merge_proj",
    )(xs[0], xs[1], hf[0], hf[1], hb[0], hb[1], attn[0], attn[1], proj, proj, proj, bg, wr, wa, wo, n2, wq)


def _topk_rows(s, k, ids=None):
    if ids is None:
        ids = lax.broadcasted_iota(jnp.int32, s.shape, 0).astype(F32)
    slot = lax.broadcasted_iota(jnp.int32, (k, s.shape[1]), 0)
    vals = jnp.zeros((k, s.shape[1]), F32)
    idxs = jnp.zeros((k, s.shape[1]), F32)
    for t in range(k):
        m = jnp.max(s, axis=0, keepdims=True)
        pos = jnp.min(jnp.where(s == m, ids, jnp.inf), axis=0, keepdims=True)
        vals = jnp.where(slot == t, m, vals)
        idxs = jnp.where(slot == t, pos, idxs)
        s = jnp.where(ids == pos, -jnp.inf, s)
    return vals, idxs


_PAIRS = [(a, b) for a in range(PEER_TOPK) for b in range(PEER_TOPK) if (a + 1) * (b + 1) <= PEER_TOPK]


def _rows_from(src, picks, fill):
    if all(p is not None for p in picks) and picks[0] % SUBLANES == 0 and picks == list(range(picks[0], picks[0] + SUBLANES)):
        return src[picks[0]:picks[0] + SUBLANES, :]
    row = lax.broadcasted_iota(jnp.int32, (SUBLANES, src.shape[1]), 0)
    out = jnp.full((SUBLANES, src.shape[1]), fill, F32)
    for r, p in enumerate(picks):
        if p is not None:
            out = jnp.where(row == r, src[p:p + 1, :], out)
    return out


def _pair_candidates(sv1, sv2):
    pairs = _PAIRS + [None] * (-len(_PAIRS) % SUBLANES)
    width = sv1.shape[1]
    row = lax.broadcasted_iota(jnp.int32, (SUBLANES, width), 0)
    sums, ids = [], []
    for g0 in range(0, len(pairs), SUBLANES):
        grp = pairs[g0:g0 + SUBLANES]
        a_rows = _rows_from(sv1, [None if pr is None else pr[0] for pr in grp], -jnp.inf)
        b_rows = _rows_from(sv2, [None if pr is None else pr[1] for pr in grp], 0.0)
        sums.append(a_rows + b_rows)
        idv = jnp.full((SUBLANES, width), float(PEER_TOPK * PEER_TOPK), F32)
        for r, pr in enumerate(grp):
            if pr is not None:
                idv = jnp.where(row == r, float(pr[0] * PEER_TOPK + pr[1]), idv)
        ids.append(idv)
    return jnp.concatenate(sums, axis=0), jnp.concatenate(ids, axis=0)


def _router_body(qp_ref, keys_ref, idx_ref, g_ref):
    slot = lax.broadcasted_iota(jnp.int32, (PEER_TOPK, qp_ref.shape[0]), 0)
    for h in range(PEER_HEADS):
        sv, si = [], []
        for p in range(2):
            c0 = (h * 2 + p) * D_HALF
            q = qp_ref[:, c0:c0 + D_HALF].astype(BF16)
            s = lax.dot_general(keys_ref[h, p], q, (((1,), (1,)), ((), ())), preferred_element_type=F32)
            v, ix = _topk_rows(s, PEER_TOPK)
            sv.append(v)
            si.append(ix)
        cand, cand_id = _pair_candidates(sv[0], sv[1])
        top_s, pos = _topk_rows(cand, PEER_TOPK, cand_id)
        pos = pos.astype(jnp.int32)
        pa = pos >> 4
        pb = pos & (PEER_TOPK - 1)
        idx = jnp.zeros(slot.shape, F32)
        for t in range(PEER_TOPK):
            i1 = jnp.sum(jnp.where(slot == pa[t:t + 1, :], si[0], 0.0), axis=0, keepdims=True)
            i2 = jnp.sum(jnp.where(slot == pb[t:t + 1, :], si[1], 0.0), axis=0, keepdims=True)
            idx = jnp.where(slot == t, (i1 * N_KEYS + i2) * ROW_SUB, idx)
        idx = idx.astype(jnp.int32)
        e = jnp.exp(top_s - jnp.max(top_s, axis=0, keepdims=True))
        g = e / jnp.sum(e, axis=0, keepdims=True)
        idx_ref[h * PEER_TOPK:(h + 1) * PEER_TOPK, :] = idx
        g_ref[h * PEER_TOPK:(h + 1) * PEER_TOPK, :] = g


def _router(qp, keys):
    t = qp.shape[0]
    c = 256
    qw = PEER_HEADS * D_KEY
    return pl.pallas_call(
        _router_body,
        grid=(t // c,),
        in_specs=[pl.BlockSpec((c, qw), lambda i: (i, 0)),
                  pl.BlockSpec((PEER_HEADS, 2, N_KEYS, D_HALF), lambda i: (0, 0, 0, 0))],
        out_specs=[pl.BlockSpec((N_SEL, c), lambda i: (0, i)), pl.BlockSpec((N_SEL, c), lambda i: (0, i))],
        out_shape=[jax.ShapeDtypeStruct((N_SEL, t), jnp.int32), jax.ShapeDtypeStruct((N_SEL, t), F32)],
        compiler_params=pltpu.CompilerParams(dimension_semantics=("parallel",)),
        name="peer_router",
    )(qp, keys)


def _pack_body(t_ref, o_ref):
    o_ref[...] = pltpu.bitcast(t_ref[...].astype(BF16), jnp.int32)


def _pack_table(tab2):
    n8 = tab2.shape[0]
    r = 4096
    return pl.pallas_call(
        _pack_body,
        grid=(n8 // r,),
        in_specs=[pl.BlockSpec((r, LANES), lambda i: (i, 0))],
        out_specs=pl.BlockSpec((r // 2, LANES), lambda i: (i, 0)),
        out_shape=jax.ShapeDtypeStruct((n8 // 2, LANES), jnp.int32),
        compiler_params=pltpu.CompilerParams(dimension_semantics=("parallel",)),
        name="peer_pack",
    )(tab2)


PEER_UNROLL = SUBLANES
PEER_U_BLOCK = 128
PEER_V_BLOCK = 256


def _load_table(tab_hbm, tab_s, sem):
    @pl.when(pl.program_id(0) == 0)
    def _():
        cp = pltpu.make_async_copy(tab_hbm, tab_s, sem)
        cp.start()
        cp.wait()


def _gather_rows(tab_s, idx_ref, t, m_ref):
    for j in range(N_SEL):
        r = pl.multiple_of(idx_ref[t, j], ROW_SUB)
        m_ref[j * ROW_SUB:(j + 1) * ROW_SUB, :] = tab_s[pl.ds(r, ROW_SUB), :]


def _split3_bf16(x):
    hi = x.astype(BF16).astype(F32)
    r1 = x - hi
    mid = r1.astype(BF16).astype(F32)
    lo = r1 - mid
    return jnp.concatenate([hi, mid, lo, jnp.zeros_like(x)], axis=0).astype(BF16)


def _sum3(y):
    return y[0:SUBLANES] + y[SUBLANES:2 * SUBLANES] + y[2 * SUBLANES:3 * SUBLANES]


def _diag_mask():
    p = lax.broadcasted_iota(jnp.int32, (SUBLANES, N_SEL * SUBLANES), 0)
    c = lax.broadcasted_iota(jnp.int32, (SUBLANES, N_SEL * SUBLANES), 1)
    return (c & (SUBLANES - 1)) == p


def _pipelined_groups(ngroups, gather, compute):
    assert ngroups >= 2 and ngroups % 2 == 0
    gather(0, 0)

    def pair(p, carry):
        g = 2 * p
        gather(g + 1, 1)
        compute(g, 0)
        gather(g + 2, 0)
        compute(g + 1, 1)
        return carry

    lax.fori_loop(0, ngroups // 2 - 1, pair, 0)
    gather(ngroups - 1, 1)
    compute(ngroups - 2, 0)
    compute(ngroups - 1, 1)


def _peer_u_body(idx_ref, x_ref, g_ref, tab_hbm, w_ref, tab_s, m_s, fold_s, p_s, s_s, sem, *, tb):
    _load_table(tab_hbm, tab_s, sem)

    @pl.when(pl.program_id(0) == 0)
    def _():
        r = lax.broadcasted_iota(jnp.int32, fold_s.shape, 0)
        c = lax.broadcasted_iota(jnp.int32, fold_s.shape, 1)
        fold_s[...] = jnp.where((r >> 3) == c, 1.0, 0.0).astype(BF16)

    diag = _diag_mask()

    def gather(gi, half):
        for u in range(PEER_UNROLL):
            _gather_rows(tab_s, idx_ref, gi * PEER_UNROLL + u, m_s.at[half * PEER_UNROLL + u])

    def compute(gi, half):
        for u in range(PEER_UNROLL):
            t = gi * PEER_UNROLL + u
            rows = pltpu.bitcast(m_s[half * PEER_UNROLL + u], BF16)
            p = _sum3(lax.dot_general(_split3_bf16(x_ref[t]), rows, NT_DIMS, preferred_element_type=F32))
            p_s[t] = jnp.where(diag, p, 0.0)

    _pipelined_groups(tb // PEER_UNROLL, gather, compute)

    pm = p_s[...].reshape(tb * SUBLANES, N_SEL * SUBLANES)
    y = jnp.dot(pm.astype(BF16), fold_s[...], preferred_element_type=F32)
    s_s[...] = y.reshape(tb, SUBLANES, N_SEL)
    s = s_s[:, 0, :]
    for q in range(1, SUBLANES):
        s = s + s_s[:, q, :]
    w_ref[...] = g_ref[...] * jax.nn.gelu(s)


def _peer_u(idx_t, x3, g_t, tab):
    t = idx_t.shape[0]
    tb = PEER_U_BLOCK
    body = functools.partial(_peer_u_body, tb=tb)
    return pl.pallas_call(
        body,
        grid=(t // tb,),
        in_specs=[pl.BlockSpec((tb, N_SEL), lambda i: (i, 0), memory_space=pltpu.SMEM),
                  pl.BlockSpec((tb, SUBLANES, LANES), lambda i: (i, 0, 0)),
                  pl.BlockSpec((tb, N_SEL), lambda i: (i, 0)),
                  pl.BlockSpec(memory_space=pl.ANY)],
        out_specs=pl.BlockSpec((tb, N_SEL), lambda i: (i, 0)),
        out_shape=jax.ShapeDtypeStruct((t, N_SEL), F32),
        scratch_shapes=[pltpu.VMEM((N_EXPERTS * ROW_SUB, LANES), jnp.int32),
                        pltpu.VMEM((2 * PEER_UNROLL, N_SEL * ROW_SUB, LANES), jnp.int32),
                        pltpu.VMEM((N_SEL * SUBLANES, N_SEL), BF16),
                        pltpu.VMEM((tb, SUBLANES, N_SEL * SUBLANES), F32),
                        pltpu.VMEM((tb, SUBLANES, LANES), F32),
                        pltpu.SemaphoreType.DMA(())],
        compiler_params=pltpu.CompilerParams(dimension_semantics=("arbitrary",),
                                             vmem_limit_bytes=VMEM_LIMIT),
        name="peer_u",
    )(idx_t, x3, g_t, tab)


def _peer_v_body(idx_ref, w_ref, tab_hbm, o_ref, tab_s, m_s, spread_s, sem, *, tb):
    _load_table(tab_hbm, tab_s, sem)

    @pl.when(pl.program_id(0) == 0)
    def _():
        r = lax.broadcasted_iota(jnp.int32, spread_s.shape, 0)
        c = lax.broadcasted_iota(jnp.int32, spread_s.shape, 1)
        spread_s[...] = jnp.where((c >> 3) == r, 1.0, 0.0).astype(BF16)

    diag = _diag_mask()
    zero = jnp.zeros((SUBLANES, N_SEL * SUBLANES), F32)

    def gather(gi, half):
        for u in range(PEER_UNROLL):
            _gather_rows(tab_s, idx_ref, gi * PEER_UNROLL + u, m_s.at[half * PEER_UNROLL + u])

    def compute(gi, half):
        base = pl.multiple_of(gi * PEER_UNROLL, PEER_UNROLL)
        e = jnp.dot(_split3_bf16(w_ref[pl.ds(base, PEER_UNROLL), :]), spread_s[...], preferred_element_type=F32)
        for u in range(PEER_UNROLL):
            rows = pltpu.bitcast(m_s[half * PEER_UNROLL + u], BF16)
            terms = [jnp.where(diag, jnp.broadcast_to(e[SUBLANES * k + u:SUBLANES * k + u + 1, :], diag.shape), 0.0)
                     for k in range(3)]
            lhs = jnp.concatenate(terms + [zero], axis=0).astype(BF16)
            o_ref[base + u] = _sum3(jnp.dot(lhs, rows, preferred_element_type=F32))

    _pipelined_groups(tb // PEER_UNROLL, gather, compute)


def _peer_v(idx_t, w, tab):
    t = idx_t.shape[0]
    tb = PEER_V_BLOCK
    body = functools.partial(_peer_v_body, tb=tb)
    return pl.pallas_call(
        body,
        grid=(t // tb,),
        in_specs=[pl.BlockSpec((tb, N_SEL), lambda i: (i, 0), memory_space=pltpu.SMEM),
                  pl.BlockSpec((tb, N_SEL), lambda i: (i, 0)),
                  pl.BlockSpec(memory_space=pl.ANY)],
        out_specs=pl.BlockSpec((tb, SUBLANES, LANES), lambda i: (i, 0, 0)),
        out_shape=jax.ShapeDtypeStruct((t, SUBLANES, LANES), F32),
        scratch_shapes=[pltpu.VMEM((N_EXPERTS * ROW_SUB, LANES), jnp.int32),
                        pltpu.VMEM((2 * PEER_UNROLL, N_SEL * ROW_SUB, LANES), jnp.int32),
                        pltpu.VMEM((N_SEL, N_SEL * SUBLANES), BF16),
                        pltpu.SemaphoreType.DMA(())],
        compiler_params=pltpu.CompilerParams(dimension_semantics=("arbitrary",),
                                             vmem_limit_bytes=VMEM_LIMIT),
        name="peer_v",
    )(idx_t, w, tab)


def _final_body(x1_ref, o_ref, g_ref, y_ref):
    y_ref[...] = _rms(x1_ref[...] + o_ref[...], g_ref[...])


def _final(x1, o, g):
    t = x1.shape[0]
    tm = 512
    row = pl.BlockSpec((tm, D_MODEL), lambda i: (i, 0))
    return pl.pallas_call(
        _final_body,
        grid=(t // tm,),
        in_specs=[row, row, pl.BlockSpec((1, D_MODEL), lambda i: (0, 0))],
        out_specs=row,
        out_shape=jax.ShapeDtypeStruct((t, D_MODEL), F32),
        compiler_params=pltpu.CompilerParams(dimension_semantics=("parallel",)),
        name="final_norm",
    )(x1, o, g)


def _rel_bucket(rel):
    nb = NUM_BUCKETS // 2
    ret = jnp.where(rel > 0, nb, 0).astype(jnp.int32)
    n = jnp.abs(rel)
    max_exact = nb // 2
    nf = jnp.maximum(n, 1).astype(F32)
    large = max_exact + (jnp.log(nf / max_exact) / math.log(MAX_DISTANCE / max_exact) * (nb - max_exact)).astype(jnp.int32)
    large = jnp.minimum(large, nb - 1)
    return ret + jnp.where(n < max_exact, n, large)


def _bias_tables(rel_bias, tq):
    assert tq >= MAX_DISTANCE
    m = jnp.arange(2 * tq - 1, dtype=jnp.int32)
    rel = jnp.stack([d * tq + m - (tq - 1) for d in range(-BIAS_REACH, BIAS_REACH + 1)])
    w = jnp.transpose(rel_bias[_rel_bucket(rel)], (2, 0, 1)).astype(F32) * LOG2E
    period = 2 * tq
    v = jnp.concatenate([w[..., tq - 1::-1], jnp.zeros(w.shape[:2] + (1,), F32), w[..., :tq - 1:-1]], axis=-1)
    rep = jnp.tile(v, (1, 1, tq))[..., :tq * (period - 1)]
    return rep.reshape(w.shape[:2] + (tq, period - 1))[..., :tq]


def kernel(x_prompt, x_sample, norm1_g, w_in, b_gate, conv_w, conv_b, lru_wa_f, lru_ba_f, lru_wx_f, lru_bx_f, lru_lam_f, lru_wa_b, lru_ba_b, lru_wx_b, lru_bx_b, lru_lam_b, lam_q1, lam_k1, lam_q2, lam_k2, subln_g, rel_bias, w_rnn_out, w_attn_out, w_out, norm2_g, peer_wq, peer_keys, peer_u, peer_v, final_g):
    groups = [x_prompt.shape[:2], x_sample.shape[:2]]
    xs = (x_prompt.reshape(-1, D_MODEL), x_sample.reshape(-1, D_MODEL))
    t = xs[0].shape[0] + xs[1].shape[0]
    l = 0
    row = lambda a: a.reshape(1, -1)

    w_main = jnp.concatenate([w_in[l][:, :QKV_COL0], w_in[l][:, QKV_COL1:]], axis=1).astype(BF16)
    proj = _inproj(xs, row(norm1_g[l]), w_main)
    qk, vt = _qkv(xs, row(norm1_g[l]), w_in[l][:, QKV_COL0:QKV_COL1].astype(BF16))

    btiles = _bias_tables(rel_bias, 512)
    lamv = jnp.stack([lam_q1[l], lam_k1[l], lam_q2[l], lam_k2[l]])
    lru_f = (conv_w[l], row(conv_b[l]), lru_wa_f[l].astype(BF16), row(lru_ba_f[l]),
             lru_wx_f[l].astype(BF16), row(lru_bx_f[l]), row(lru_lam_f[l]))
    lru_b = (conv_w[l], row(conv_b[l]), lru_wa_b[l].astype(BF16), row(lru_ba_b[l]),
             lru_wx_b[l].astype(BF16), row(lru_bx_b[l]), row(lru_lam_b[l]))
    hf, hb, attn = [], [], []
    row0 = 0
    for nb, s in groups:
        hf.append(_lru(proj, row0, nb, s, *lru_f, reverse=False))
        hb.append(_lru(proj, row0, nb, s, *lru_b, reverse=True))
        attn.append(_attention(qk, vt, row0, nb, s, btiles, lamv, row(subln_g[l])))
        row0 += nb * s

    x1, xn2, qp = _merge(xs, hf, hb, proj, attn, b_gate[l].reshape(2, D_MODEL),
                         w_rnn_out[l].astype(BF16), w_attn_out[l].astype(BF16), w_out[l].astype(BF16),
                         row(norm2_g[l]), peer_wq[l].astype(BF16))

    idx, g = _router(qp, peer_keys[l].astype(BF16))
    idx_t = idx.T
    g_t = g.T
    tab_u = _pack_table(peer_u[l].reshape(N_EXPERTS * SUBLANES, LANES))
    tab_v = _pack_table(peer_v[l].reshape(N_EXPERTS * SUBLANES, LANES))
    w = _peer_u(idx_t, xn2.reshape(t, SUBLANES, LANES), g_t, tab_u)
    o = _peer_v(idx_t, w, tab_v).reshape(t, D_MODEL)

    y = _final(x1, o, row(final_g))
    n0 = groups[0][0] * groups[0][1]
    return (y[:n0].reshape(x_prompt.shape), y[n0:].reshape(x_sample.shape))
```

```python
import functools
import math

import jax
import jax.numpy as jnp
from jax import lax
from jax.experimental import pallas as pl
from jax.experimental.pallas import tpu as pltpu

F32 = jnp.float32
BF16 = jnp.bfloat16

D_MODEL = 1024
D_RNN = 1024
LRU_BLOCKS = 4
LRU_BW = D_RNN // LRU_BLOCKS
LRU_C = 8.0
CONV_W = 4
N_HEADS = 8
HEAD_DIM = 64
V_DIM = 2 * HEAD_DIM
V_AUG = V_DIM + 16
ATTN_W = N_HEADS * V_DIM
NUM_BUCKETS = 32
MAX_DISTANCE = 128
PEER_HEADS = 8
N_KEYS = 128
N_EXPERTS = N_KEYS * N_KEYS
PEER_TOPK = 16
D_KEY = 256
D_HALF = D_KEY // 2
N_SEL = PEER_HEADS * PEER_TOPK
IN_W = 2 * D_RNN + 3 * ATTN_W + 2 * D_MODEL
EPS = 1e-6
LAM_INIT = 0.8 - 0.6 * math.exp(-0.3 * 0)
LOG2E = math.log2(math.e)
NT_DIMS = (((1,), (1,)), ((), ()))
BIAS_REACH = 2

COL_XRNN, COL_GRNN, COL_GLR, COL_GLA = range(4)
QKV_COL0 = 2 * D_RNN
QKV_COL1 = QKV_COL0 + 3 * ATTN_W

SUBLANES = 8
LANES = 128
ROW_WORDS = D_MODEL // 2
ROW_SUB = ROW_WORDS // LANES
VMEM_LIMIT = 56 * 1024 * 1024


def _rms(x, g):
    return x * lax.rsqrt(jnp.mean(x * x, axis=-1, keepdims=True) + EPS) * g


def _group_rows(tm, nblk0, nblk1):
    return (lambda i, *_: (jnp.minimum(i, nblk0 - 1), 0),
            lambda i, *_: (jnp.clip(i - nblk0, 0, nblk1 - 1), 0))


def _inproj_body(x0_ref, x1_ref, g_ref, w_ref, o_ref, xn_ref, *, nblk0):
    @pl.when(pl.program_id(1) == 0)
    def _():
        x = jnp.where(pl.program_id(0) < nblk0, x0_ref[...], x1_ref[...])
        xn_ref[...] = _rms(x, g_ref[...]).astype(BF16)

    o_ref[...] = jnp.dot(xn_ref[...], w_ref[...], preferred_element_type=F32)


def _inproj(xs, g, w):
    n_out = w.shape[1]
    tm, tn = 1024, 1024
    nblk0, nblk1 = xs[0].shape[0] // tm, xs[1].shape[0] // tm
    rows0, rows1 = _group_rows(tm, nblk0, nblk1)
    return pl.pallas_call(
        functools.partial(_inproj_body, nblk0=nblk0),
        grid=(nblk0 + nblk1, n_out // tn),
        in_specs=[
            pl.BlockSpec((tm, D_MODEL), rows0),
            pl.BlockSpec((tm, D_MODEL), rows1),
            pl.BlockSpec((1, D_MODEL), lambda i, j: (0, 0)),
            pl.BlockSpec((D_MODEL, tn), lambda i, j: (0, j)),
        ],
        out_specs=pl.BlockSpec((tm, tn), lambda i, j: (i, j)),
        out_shape=jax.ShapeDtypeStruct(((nblk0 + nblk1) * tm, n_out), F32),
        scratch_shapes=[pltpu.VMEM((tm, D_MODEL), BF16)],
        compiler_params=pltpu.CompilerParams(dimension_semantics=("parallel", "arbitrary")),
        name="inproj",
    )(xs[0], xs[1], g, w)


def _qkv_body(x0_ref, x1_ref, g_ref, w_ref, qk_ref, vt_ref, xn_ref, *, nblk0):
    j = pl.program_id(1)

    @pl.when(j == 0)
    def _():
        x = jnp.where(pl.program_id(0) < nblk0, x0_ref[...], x1_ref[...])
        xn_ref[...] = _rms(x, g_ref[...]).astype(BF16)

    res = jnp.dot(xn_ref[...], w_ref[...], preferred_element_type=F32)

    @pl.when(j < 2)
    def _():
        for h in range(N_HEADS):
            qk_ref[h] = res[:, h * V_DIM:(h + 1) * V_DIM].astype(BF16)

    @pl.when(j == 2)
    def _():
        for h in range(N_HEADS):
            vt_ref[h, 0:V_DIM, :] = res[:, h * V_DIM:(h + 1) * V_DIM].T.astype(BF16)
            vt_ref[h, V_DIM:V_AUG, :] = jnp.ones((V_AUG - V_DIM, res.shape[0]), BF16)


def _qkv(xs, g, w):
    tm = 1024
    nblk0, nblk1 = xs[0].shape[0] // tm, xs[1].shape[0] // tm
    t = (nblk0 + nblk1) * tm
    rows0, rows1 = _group_rows(tm, nblk0, nblk1)
    return pl.pallas_call(
        functools.partial(_qkv_body, nblk0=nblk0),
        grid=(nblk0 + nblk1, 3),
        in_specs=[
            pl.BlockSpec((tm, D_MODEL), rows0),
            pl.BlockSpec((tm, D_MODEL), rows1),
            pl.BlockSpec((1, D_MODEL), lambda i, j: (0, 0)),
            pl.BlockSpec((D_MODEL, ATTN_W), lambda i, j: (0, j)),
        ],
        out_specs=[pl.BlockSpec((None, N_HEADS, tm, V_DIM), lambda i, j: (jnp.minimum(j, 1), 0, i, 0)),
                   pl.BlockSpec((N_HEADS, V_AUG, tm), lambda i, j: (0, 0, i))],
        out_shape=[jax.ShapeDtypeStruct((2, N_HEADS, t, V_DIM), BF16),
                   jax.ShapeDtypeStruct((N_HEADS, V_AUG, t), BF16)],
        scratch_shapes=[pltpu.VMEM((tm, D_MODEL), BF16)],
        compiler_params=pltpu.CompilerParams(dimension_semantics=("parallel", "arbitrary")),
        name="qkv_proj",
    )(xs[0], xs[1], g, w)


def _lru_body(xc_ref, xp_ref, xn_ref, cw_ref, cb_ref, wa_ref, ba_ref, wx_ref, bx_ref, lam_ref,
              h_ref, a_s, b_s, carry, *, reverse, nchunks, tc):
    i = pl.program_id(1)
    c = (nchunks - 1 - i) if reverse else i
    x = xc_ref[...]
    rows = lax.broadcasted_iota(jnp.int32, (tc, D_RNN), 0)
    prev = jnp.where(c > 0, xp_ref[SUBLANES - 1:SUBLANES, :], 0.0)
    nxt0 = jnp.where(c < nchunks - 1, xn_ref[0:1, :], 0.0)
    nxt1 = jnp.where(c < nchunks - 1, xn_ref[1:2, :], 0.0)
    xm1 = jnp.where(rows == 0, prev, pltpu.roll(x, 1, 0))
    xp1 = jnp.where(rows == tc - 1, nxt0, pltpu.roll(x, tc - 1, 0))
    xp2 = jnp.where(rows == tc - 2, nxt0, jnp.where(rows == tc - 1, nxt1, pltpu.roll(x, tc - 2, 0)))
    cw = cw_ref[...]
    xc = cb_ref[...] + xm1 * cw[0:1] + x * cw[1:2] + xp1 * cw[2:3] + xp2 * cw[3:4]

    xcb = xc.astype(BF16)
    r_parts, i_parts = [], []
    for n in range(LRU_BLOCKS):
        xs = xcb[:, n * LRU_BW:(n + 1) * LRU_BW]
        r_parts.append(jnp.dot(xs, wa_ref[n], preferred_element_type=F32))
        i_parts.append(jnp.dot(xs, wx_ref[n], preferred_element_type=F32))
    r = 0.5 * jnp.tanh(0.5 * (jnp.concatenate(r_parts, axis=1) + ba_ref[...])) + 0.5
    ig = 0.5 * jnp.tanh(0.5 * (jnp.concatenate(i_parts, axis=1) + bx_ref[...])) + 0.5
    z = -lam_ref[...]
    softplus = jnp.maximum(z, 0.0) + jnp.log1p(jnp.exp(-jnp.abs(z)))
    log_a = -LRU_C * r * softplus
    a = jnp.exp(log_a)
    mult = jnp.sqrt(-jnp.tanh(log_a) * (a * a + 1.0))
    edge_row = tc - 1 if reverse else 0
    edge_chunk = nchunks - 1 if reverse else 0
    mult = jnp.where(rows == jnp.where(c == edge_chunk, edge_row, -1), 1.0, mult)
    a_s[...] = a
    b_s[...] = mult * ig * xc

    row8 = lax.broadcasted_iota(jnp.int32, (SUBLANES, D_RNN), 0)
    ngroups = tc // SUBLANES
    h0 = jnp.where(i == 0, 0.0, carry[...])

    def step(gi, h):
        g = (ngroups - 1 - gi) if reverse else gi
        off = pl.multiple_of(g * SUBLANES, SUBLANES)
        av = a_s[pl.ds(off, SUBLANES), :]
        bv = b_s[pl.ds(off, SUBLANES), :]
        for s in (1, 2, 4):
            if reverse:
                keep = row8 < SUBLANES - s
                shift = SUBLANES - s
            else:
                keep = row8 >= s
                shift = s
            a_sh = jnp.where(keep, pltpu.roll(av, shift, 0), 1.0)
            b_sh = jnp.where(keep, pltpu.roll(bv, shift, 0), 0.0)
            bv = av * b_sh + bv
            av = av * a_sh
        hv = av * h + bv
        h_ref[pl.ds(off, SUBLANES), :] = hv
        last = hv[0:1, :] if reverse else hv[SUBLANES - 1:SUBLANES, :]
        return jnp.broadcast_to(last, (SUBLANES, D_RNN))

    carry[...] = lax.fori_loop(0, ngroups, step, h0)


def _lru(proj, row0, nb, s, cw, cb, wa, ba, wx, bx, lam, reverse):
    tc = 512
    nchunks = s // tc
    t_all = proj.shape[0]
    blk0 = row0 // tc
    last8 = t_all // SUBLANES - 1

    def cidx(i):
        return (nchunks - 1 - i) if reverse else i

    def cur(b, i):
        return (blk0 + b * nchunks + cidx(i), COL_XRNN)

    def prev8(b, i):
        r = (blk0 + b * nchunks + cidx(i)) * (tc // SUBLANES) - 1
        return (jnp.maximum(r, 0), COL_XRNN)

    def next8(b, i):
        r = (blk0 + b * nchunks + cidx(i) + 1) * (tc // SUBLANES)
        return (jnp.minimum(r, last8), COL_XRNN)

    full = lambda shape: pl.BlockSpec(shape, lambda b, i: (0,) * len(shape))
    body = functools.partial(_lru_body, reverse=reverse, nchunks=nchunks, tc=tc)
    return pl.pallas_call(
        body,
        grid=(nb, nchunks),
        in_specs=[
            pl.BlockSpec((tc, D_RNN), cur),
            pl.BlockSpec((SUBLANES, D_RNN), prev8),
            pl.BlockSpec((SUBLANES, D_RNN), next8),
            full((CONV_W, D_RNN)), full((1, D_RNN)),
            full((LRU_BLOCKS, LRU_BW, LRU_BW)), full((1, D_RNN)),
            full((LRU_BLOCKS, LRU_BW, LRU_BW)), full((1, D_RNN)),
            full((1, D_RNN)),
        ],
        out_specs=pl.BlockSpec((tc, D_RNN), lambda b, i: (b * nchunks + cidx(i), 0)),
        out_shape=jax.ShapeDtypeStruct((nb * s, D_RNN), F32),
        scratch_shapes=[pltpu.VMEM((tc, D_RNN), F32), pltpu.VMEM((tc, D_RNN), F32),
                        pltpu.VMEM((SUBLANES, D_RNN), F32)],
        compiler_params=pltpu.CompilerParams(dimension_semantics=("arbitrary", "arbitrary")),
        name="lru_bwd" if reverse else "lru_fwd",
    )(proj, proj, proj, cw, cb, wa, ba, wx, bx, lam)


def _attn_body(q_ref, k_ref, vt_ref, bt_ref, lamv_ref, g_ref, o_ref,
               q1_s, q2_s, m1_s, acc1_s, m2_s, acc2_s, sa_s, sb_s, *, nkv, tkv):
    i = pl.program_id(2)

    q = q_ref[...].astype(F32) * (HEAD_DIM ** -0.5 * LOG2E)
    lane = lax.broadcasted_iota(jnp.int32, q.shape, 1)
    q1_s[...] = jnp.where(lane < HEAD_DIM, q, 0.0).astype(BF16)
    q2_s[...] = jnp.where(lane >= HEAD_DIM, q, 0.0).astype(BF16)
    for m_s, acc_s in ((m1_s, acc1_s), (m2_s, acc2_s)):
        m_s[...] = jnp.full(m_s.shape, -jnp.inf, F32)
        acc_s[...] = jnp.zeros(acc_s.shape, F32)

    def scores(j, buf):
        off = pl.multiple_of(j * tkv, tkv)
        k = k_ref[pl.ds(off, tkv), :]
        bias = bt_ref[jnp.clip(j - i, -BIAS_REACH, BIAS_REACH) + BIAS_REACH]
        for c, q_s in enumerate((q1_s, q2_s)):
            buf[c] = lax.dot_general(k, q_s[...], NT_DIMS, preferred_element_type=F32) + bias

    def accumulate(j, buf):
        off = pl.multiple_of(j * tkv, tkv)
        vt = vt_ref[:, pl.ds(off, tkv)]
        for c, (m_s, acc_s) in enumerate(((m1_s, acc1_s), (m2_s, acc2_s))):
            s = buf[c]
            m_old = m_s[...]
            m_new = jnp.maximum(m_old, jnp.max(s, axis=0, keepdims=True))
            alpha = jnp.exp2(m_old - m_new)
            p = jnp.exp2((s - m_new).astype(BF16))
            acc_s[...] = alpha * acc_s[...] + jnp.dot(vt, p, preferred_element_type=F32)
            m_s[...] = m_new

    scores(0, sa_s)

    def pair(u, carry):
        j = 2 * u
        scores(j + 1, sb_s)
        accumulate(j, sa_s)
        scores(j + 2, sa_s)
        accumulate(j + 1, sb_s)
        return carry

    lax.fori_loop(0, nkv // 2 - 1, pair, 0)
    scores(nkv - 1, sb_s)
    accumulate(nkv - 2, sa_s)
    accumulate(nkv - 1, sb_s)

    lv = lamv_ref[...]
    lam = (jnp.exp(jnp.sum(lv[0:1] * lv[1:2], axis=-1, keepdims=True))
           - jnp.exp(jnp.sum(lv[2:3] * lv[3:4], axis=-1, keepdims=True)) + LAM_INIT)
    o = (acc1_s[0:V_DIM, :] / acc1_s[V_DIM:V_DIM + 1, :]
         - lam * (acc2_s[0:V_DIM, :] / acc2_s[V_DIM:V_DIM + 1, :]))
    o_ref[...] = _rms(o.T, g_ref[...]) * (1.0 - LAM_INIT)


def _attention(qk, vt, row0, nb, s, btiles, lamv, subln_g):
    tq = btiles.shape[-1]
    nq = s // tq
    assert nq >= 2 and nq % 2 == 0
    blk0 = row0 // tq
    seq0 = row0 // s

    body = functools.partial(_attn_body, nkv=nq, tkv=tq)
    stat = pltpu.VMEM((1, tq), F32)
    acc = pltpu.VMEM((V_AUG, tq), F32)
    sbuf = pltpu.VMEM((2, tq, tq), F32)
    return pl.pallas_call(
        body,
        grid=(nb, N_HEADS, nq),
        in_specs=[
            pl.BlockSpec((None, None, tq, V_DIM), lambda b, h, i: (0, h, blk0 + b * nq + i, 0)),
            pl.BlockSpec((None, None, s, V_DIM), lambda b, h, i: (1, h, seq0 + b, 0)),
            pl.BlockSpec((None, V_AUG, s), lambda b, h, i: (h, 0, seq0 + b)),
            pl.BlockSpec((None, 2 * BIAS_REACH + 1, tq, tq), lambda b, h, i: (h, 0, 0, 0)),
            pl.BlockSpec((4, HEAD_DIM), lambda b, h, i: (0, 0)),
            pl.BlockSpec((1, V_DIM), lambda b, h, i: (0, 0)),
        ],
        out_specs=pl.BlockSpec((tq, V_DIM), lambda b, h, i: (b * nq + i, h)),
        out_shape=jax.ShapeDtypeStruct((nb * s, ATTN_W), F32),
        scratch_shapes=[pltpu.VMEM((tq, V_DIM), BF16), pltpu.VMEM((tq, V_DIM), BF16),
                        stat, acc, stat, acc, sbuf, sbuf],
        compiler_params=pltpu.CompilerParams(
            dimension_semantics=("parallel", "parallel", "arbitrary"),
            vmem_limit_bytes=VMEM_LIMIT),
        name="diff_attn",
    )(qk, qk, vt, btiles, lamv, subln_g)


def _merge_body(xg0_ref, xg1_ref, hf0_ref, hf1_ref, hb0_ref, hb1_ref, at0_ref, at1_ref, grnn_ref, glr_ref, gla_ref, bg_ref,
                wr_ref, wa_ref, wo_ref, n2_ref, wq_ref, x1_ref, xn2_ref, qp_ref, *, nblk0):
    first = pl.program_id(0) < nblk0
    x = jnp.where(first, xg0_ref[...], xg1_ref[...])
    hf = jnp.where(first, hf0_ref[...], hf1_ref[...])
    hb = jnp.where(first, hb0_ref[...], hb1_ref[...])
    attn = jnp.where(first, at0_ref[...], at1_ref[...])
    hg = ((hf + hb) * jax.nn.gelu(grnn_ref[...])).astype(BF16)
    y_rnn = jnp.dot(hg, wr_ref[...], preferred_element_type=F32)
    y_attn = jnp.dot(attn.astype(BF16), wa_ref[...], preferred_element_type=F32)
    bg = bg_ref[...]
    g_r = jax.nn.sigmoid(glr_ref[...] + bg[0:1])
    g_a = jax.nn.sigmoid(gla_ref[...] + bg[1:2])
    merged = (g_r * y_rnn + g_a * y_attn).astype(BF16)
    x1 = x + jnp.dot(merged, wo_ref[...], preferred_element_type=F32)
    x1_ref[...] = x1
    xn2 = _rms(x1, n2_ref[...])
    xn2_ref[...] = xn2
    qp_ref[...] = jnp.dot(xn2.astype(BF16), wq_ref[...], preferred_element_type=F32)


def _merge(xs, hf, hb, proj, attn, bg, wr, wa, wo, n2, wq):
    tm = 256
    nblk0 = hf[0].shape[0] // tm
    nblk1 = hf[1].shape[0] // tm
    t = (nblk0 + nblk1) * tm
    row = lambda c: pl.BlockSpec((tm, D_MODEL), lambda i: (i, c))
    rows0, rows1 = _group_rows(tm, nblk0, nblk1)
    grp0 = pl.BlockSpec((tm, D_MODEL), rows0)
    grp1 = pl.BlockSpec((tm, D_MODEL), rows1)
    full = lambda shape: pl.BlockSpec(shape, lambda i: (0,) * len(shape))
    qw = PEER_HEADS * D_KEY
    return pl.pallas_call(
        functools.partial(_merge_body, nblk0=nblk0),
        grid=(t // tm,),
        in_specs=[grp0, grp1, grp0, grp1, grp0, grp1, grp0, grp1, row(COL_GRNN), row(COL_GLR), row(COL_GLA),
                  full((2, D_MODEL)), full((D_RNN, D_MODEL)), full((ATTN_W, D_MODEL)),
                  full((D_MODEL, D_MODEL)), full((1, D_MODEL)), full((D_MODEL, qw))],
        out_specs=[row(0), row(0), pl.BlockSpec((tm, qw), lambda i: (i, 0))],
        out_shape=[jax.ShapeDtypeStruct((t, D_MODEL), F32), jax.ShapeDtypeStruct((t, D_MODEL), F32),
                   jax.ShapeDtypeStruct((t, qw), F32)],
        compiler_params=pltpu.CompilerParams(dimension_semantics=("parallel",),
                                             vmem_limit_bytes=VMEM_LIMIT),
        name="merge_proj",
    )(xs[0], xs[1], hf[0], hf[1], hb[0], hb[1], attn[0], attn[1], proj, proj, proj, bg, wr, wa, wo, n2, wq)


def _topk_rows(s, k, ids=None):
    if ids is None:
        ids = lax.broadcasted_iota(jnp.int32, s.shape, 0).astype(F32)
    slot = lax.broadcasted_iota(jnp.int32, (k, s.shape[1]), 0)
    vals = jnp.zeros((k, s.shape[1]), F32)
    idxs = jnp.zeros((k, s.shape[1]), F32)
    for t in range(k):
        m = jnp.max(s, axis=0, keepdims=True)
        pos = jnp.min(jnp.where(s == m, ids, jnp.inf), axis=0, keepdims=True)
        vals = jnp.where(slot == t, m, vals)
        idxs = jnp.where(slot == t, pos, idxs)
        s = jnp.where(ids == pos, -jnp.inf, s)
    return vals, idxs


_PAIRS = [(a, b) for a in range(PEER_TOPK) for b in range(PEER_TOPK) if (a + 1) * (b + 1) <= PEER_TOPK]


def _rows_from(src, picks, fill):
    if all(p is not None for p in picks) and picks[0] % SUBLANES == 0 and picks == list(range(picks[0], picks[0] + SUBLANES)):
        return src[picks[0]:picks[0] + SUBLANES, :]
    row = lax.broadcasted_iota(jnp.int32, (SUBLANES, src.shape[1]), 0)
    out = jnp.full((SUBLANES, src.shape[1]), fill, F32)
    for r, p in enumerate(picks):
        if p is not None:
            out = jnp.where(row == r, src[p:p + 1, :], out)
    return out


def _pair_candidates(sv1, sv2):
    pairs = _PAIRS + [None] * (-len(_PAIRS) % SUBLANES)
    width = sv1.shape[1]
    row = lax.broadcasted_iota(jnp.int32, (SUBLANES, width), 0)
    sums, ids = [], []
    for g0 in range(0, len(pairs), SUBLANES):
        grp = pairs[g0:g0 + SUBLANES]
        a_rows = _rows_from(sv1, [None if pr is None else pr[0] for pr in grp], -jnp.inf)
        b_rows = _rows_from(sv2, [None if pr is None else pr[1] for pr in grp], 0.0)
        sums.append(a_rows + b_rows)
        idv = jnp.full((SUBLANES, width), float(PEER_TOPK * PEER_TOPK), F32)
        for r, pr in enumerate(grp):
            if pr is not None:
                idv = jnp.where(row == r, float(pr[0] * PEER_TOPK + pr[1]), idv)
        ids.append(idv)
    return jnp.concatenate(sums, axis=0), jnp.concatenate(ids, axis=0)


def _router_body(qp_ref, keys_ref, idx_ref, g_ref):
    slot = lax.broadcasted_iota(jnp.int32, (PEER_TOPK, qp_ref.shape[0]), 0)
    for h in range(PEER_HEADS):
        sv, si = [], []
        for p in range(2):
            c0 = (h * 2 + p) * D_HALF
            q = qp_ref[:, c0:c0 + D_HALF].astype(BF16)
            s = lax.dot_general(keys_ref[h, p], q, (((1,), (1,)), ((), ())), preferred_element_type=F32)
            v, ix = _topk_rows(s, PEER_TOPK)
            sv.append(v)
            si.append(ix)
        cand, cand_id = _pair_candidates(sv[0], sv[1])
        top_s, pos = _topk_rows(cand, PEER_TOPK, cand_id)
        pos = pos.astype(jnp.int32)
        pa = pos >> 4
        pb = pos & (PEER_TOPK - 1)
        idx = jnp.zeros(slot.shape, F32)
        for t in range(PEER_TOPK):
            i1 = jnp.sum(jnp.where(slot == pa[t:t + 1, :], si[0], 0.0), axis=0, keepdims=True)
            i2 = jnp.sum(jnp.where(slot == pb[t:t + 1, :], si[1], 0.0), axis=0, keepdims=True)
            idx = jnp.where(slot == t, (i1 * N_KEYS + i2) * ROW_SUB, idx)
        idx = idx.astype(jnp.int32)
        e = jnp.exp(top_s - jnp.max(top_s, axis=0, keepdims=True))
        g = e / jnp.sum(e, axis=0, keepdims=True)
        idx_ref[h * PEER_TOPK:(h + 1) * PEER_TOPK, :] = idx
        g_ref[h * PEER_TOPK:(h + 1) * PEER_TOPK, :] = g


def _router(qp, keys):
    t = qp.shape[0]
    c = 256
    qw = PEER_HEADS * D_KEY
    return pl.pallas_call(
        _router_body,
        grid=(t // c,),
        in_specs=[pl.BlockSpec((c, qw), lambda i: (i, 0)),
                  pl.BlockSpec((PEER_HEADS, 2, N_KEYS, D_HALF), lambda i: (0, 0, 0, 0))],
        out_specs=[pl.BlockSpec((N_SEL, c), lambda i: (0, i)), pl.BlockSpec((N_SEL, c), lambda i: (0, i))],
        out_shape=[jax.ShapeDtypeStruct((N_SEL, t), jnp.int32), jax.ShapeDtypeStruct((N_SEL, t), F32)],
        compiler_params=pltpu.CompilerParams(dimension_semantics=("parallel",)),
        name="peer_router",
    )(qp, keys)


def _pack_body(t_ref, o_ref):
    o_ref[...] = pltpu.bitcast(t_ref[...].astype(BF16), jnp.int32)


def _pack_table(tab2):
    n8 = tab2.shape[0]
    r = 4096
    return pl.pallas_call(
        _pack_body,
        grid=(n8 // r,),
        in_specs=[pl.BlockSpec((r, LANES), lambda i: (i, 0))],
        out_specs=pl.BlockSpec((r // 2, LANES), lambda i: (i, 0)),
        out_shape=jax.ShapeDtypeStruct((n8 // 2, LANES), jnp.int32),
        compiler_params=pltpu.CompilerParams(dimension_semantics=("parallel",)),
        name="peer_pack",
    )(tab2)


PEER_UNROLL = SUBLANES
PEER_U_BLOCK = 128
PEER_V_BLOCK = 256


def _load_table(tab_hbm, tab_s, sem):
    @pl.when(pl.program_id(0) == 0)
    def _():
        cp = pltpu.make_async_copy(tab_hbm, tab_s, sem)
        cp.start()
        cp.wait()


def _gather_rows(tab_s, idx_ref, t, m_ref):
    for j in range(N_SEL):
        r = pl.multiple_of(idx_ref[t, j], ROW_SUB)
        m_ref[j * ROW_SUB:(j + 1) * ROW_SUB, :] = tab_s[pl.ds(r, ROW_SUB), :]


def _split3_bf16(x):
    hi = x.astype(BF16).astype(F32)
    r1 = x - hi
    mid = r1.astype(BF16).astype(F32)
    lo = r1 - mid
    return jnp.concatenate([hi, mid, lo, jnp.zeros_like(x)], axis=0).astype(BF16)


def _sum3(y):
    return y[0:SUBLANES] + y[SUBLANES:2 * SUBLANES] + y[2 * SUBLANES:3 * SUBLANES]


def _diag_mask():
    p = lax.broadcasted_iota(jnp.int32, (SUBLANES, N_SEL * SUBLANES), 0)
    c = lax.broadcasted_iota(jnp.int32, (SUBLANES, N_SEL * SUBLANES), 1)
    return (c & (SUBLANES - 1)) == p


def _pipelined_groups(ngroups, gather, compute):
    assert ngroups >= 2 and ngroups % 2 == 0
    gather(0, 0)

    def pair(p, carry):
        g = 2 * p
        gather(g + 1, 1)
        compute(g, 0)
        gather(g + 2, 0)
        compute(g + 1, 1)
        return carry

    lax.fori_loop(0, ngroups // 2 - 1, pair, 0)
    gather(ngroups - 1, 1)
    compute(ngroups - 2, 0)
    compute(ngroups - 1, 1)


def _peer_u_body(idx_ref, x_ref, g_ref, tab_hbm, w_ref, tab_s, m_s, fold_s, p_s, s_s, sem, *, tb):
    _load_table(tab_hbm, tab_s, sem)

    @pl.when(pl.program_id(0) == 0)
    def _():
        r = lax.broadcasted_iota(jnp.int32, fold_s.shape, 0)
        c = lax.broadcasted_iota(jnp.int32, fold_s.shape, 1)
        fold_s[...] = jnp.where((r >> 3) == c, 1.0, 0.0).astype(BF16)

    diag = _diag_mask()

    def gather(gi, half):
        for u in range(PEER_UNROLL):
            _gather_rows(tab_s, idx_ref, gi * PEER_UNROLL + u, m_s.at[half * PEER_UNROLL + u])

    def compute(gi, half):
        for u in range(PEER_UNROLL):
            t = gi * PEER_UNROLL + u
            rows = pltpu.bitcast(m_s[half * PEER_UNROLL + u], BF16)
            p = _sum3(lax.dot_general(_split3_bf16(x_ref[t]), rows, NT_DIMS, preferred_element_type=F32))
            p_s[t] = jnp.where(diag, p, 0.0)

    _pipelined_groups(tb // PEER_UNROLL, gather, compute)

    pm = p_s[...].reshape(tb * SUBLANES, N_SEL * SUBLANES)
    y = jnp.dot(pm.astype(BF16), fold_s[...], preferred_element_type=F32)
    s_s[...] = y.reshape(tb, SUBLANES, N_SEL)
    s = s_s[:, 0, :]
    for q in range(1, SUBLANES):
        s = s + s_s[:, q, :]
    w_ref[...] = g_ref[...] * jax.nn.gelu(s)


def _peer_u(idx_t, x3, g_t, tab):
    t = idx_t.shape[0]
    tb = PEER_U_BLOCK
    body = functools.partial(_peer_u_body, tb=tb)
    return pl.pallas_call(
        body,
        grid=(t // tb,),
        in_specs=[pl.BlockSpec((tb, N_SEL), lambda i: (i, 0), memory_space=pltpu.SMEM),
                  pl.BlockSpec((tb, SUBLANES, LANES), lambda i: (i, 0, 0)),
                  pl.BlockSpec((tb, N_SEL), lambda i: (i, 0)),
                  pl.BlockSpec(memory_space=pl.ANY)],
        out_specs=pl.BlockSpec((tb, N_SEL), lambda i: (i, 0)),
        out_shape=jax.ShapeDtypeStruct((t, N_SEL), F32),
        scratch_shapes=[pltpu.VMEM((N_EXPERTS * ROW_SUB, LANES), jnp.int32),
                        pltpu.VMEM((2 * PEER_UNROLL, N_SEL * ROW_SUB, LANES), jnp.int32),
                        pltpu.VMEM((N_SEL * SUBLANES, N_SEL), BF16),
                        pltpu.VMEM((tb, SUBLANES, N_SEL * SUBLANES), F32),
                        pltpu.VMEM((tb, SUBLANES, LANES), F32),
                        pltpu.SemaphoreType.DMA(())],
        compiler_params=pltpu.CompilerParams(dimension_semantics=("arbitrary",),
                                             vmem_limit_bytes=VMEM_LIMIT),
        name="peer_u",
    )(idx_t, x3, g_t, tab)


def _peer_v_body(idx_ref, w_ref, tab_hbm, o_ref, tab_s, m_s, spread_s, sem, *, tb):
    _load_table(tab_hbm, tab_s, sem)

    @pl.when(pl.program_id(0) == 0)
    def _():
        r = lax.broadcasted_iota(jnp.int32, spread_s.shape, 0)
        c = lax.broadcasted_iota(jnp.int32, spread_s.shape, 1)
        spread_s[...] = jnp.where((c >> 3) == r, 1.0, 0.0).astype(BF16)

    diag = _diag_mask()
    zero = jnp.zeros((SUBLANES, N_SEL * SUBLANES), F32)

    def gather(gi, half):
        for u in range(PEER_UNROLL):
            _gather_rows(tab_s, idx_ref, gi * PEER_UNROLL + u, m_s.at[half * PEER_UNROLL + u])

    def compute(gi, half):
        base = pl.multiple_of(gi * PEER_UNROLL, PEER_UNROLL)
        e = jnp.dot(_split3_bf16(w_ref[pl.ds(base, PEER_UNROLL), :]), spread_s[...], preferred_element_type=F32)
        for u in range(PEER_UNROLL):
            rows = pltpu.bitcast(m_s[half * PEER_UNROLL + u], BF16)
            terms = [jnp.where(diag, jnp.broadcast_to(e[SUBLANES * k + u:SUBLANES * k + u + 1, :], diag.shape), 0.0)
                     for k in range(3)]
            lhs = jnp.concatenate(terms + [zero], axis=0).astype(BF16)
            o_ref[base + u] = _sum3(jnp.dot(lhs, rows, preferred_element_type=F32))

    _pipelined_groups(tb // PEER_UNROLL, gather, compute)


def _peer_v(idx_t, w, tab):
    t = idx_t.shape[0]
    tb = PEER_V_BLOCK
    body = functools.partial(_peer_v_body, tb=tb)
    return pl.pallas_call(
        body,
        grid=(t // tb,),
        in_specs=[pl.BlockSpec((tb, N_SEL), lambda i: (i, 0), memory_space=pltpu.SMEM),
                  pl.BlockSpec((tb, N_SEL), lambda i: (i, 0)),
                  pl.BlockSpec(memory_space=pl.ANY)],
        out_specs=pl.BlockSpec((tb, SUBLANES, LANES), lambda i: (i, 0, 0)),
        out_shape=jax.ShapeDtypeStruct((t, SUBLANES, LANES), F32),
        scratch_shapes=[pltpu.VMEM((N_EXPERTS * ROW_SUB, LANES), jnp.int32),
                        pltpu.VMEM((2 * PEER_UNROLL, N_SEL * ROW_SUB, LANES), jnp.int32),
                        pltpu.VMEM((N_SEL, N_SEL * SUBLANES), BF16),
                        pltpu.SemaphoreType.DMA(())],
        compiler_params=pltpu.CompilerParams(dimension_semantics=("arbitrary",),
                                             vmem_limit_bytes=VMEM_LIMIT),
        name="peer_v",
    )(idx_t, w, tab)


def _final_body(x1_ref, o_ref, g_ref, y_ref):
    y_ref[...] = _rms(x1_ref[...] + o_ref[...], g_ref[...])


def _final(x1, o, g):
    t = x1.shape[0]
    tm = 512
    row = pl.BlockSpec((tm, D_MODEL), lambda i: (i, 0))
    return pl.pallas_call(
        _final_body,
        grid=(t // tm,),
        in_specs=[row, row, pl.BlockSpec((1, D_MODEL), lambda i: (0, 0))],
        out_specs=row,
        out_shape=jax.ShapeDtypeStruct((t, D_MODEL), F32),
        compiler_params=pltpu.CompilerParams(dimension_semantics=("parallel",)),
        name="final_norm",
    )(x1, o, g)


def _rel_bucket(rel):
    nb = NUM_BUCKETS // 2
    ret = jnp.where(rel > 0, nb, 0).astype(jnp.int32)
    n = jnp.abs(rel)
    max_exact = nb // 2
    nf = jnp.maximum(n, 1).astype(F32)
    large = max_exact + (jnp.log(nf / max_exact) / math.log(MAX_DISTANCE / max_exact) * (nb - max_exact)).astype(jnp.int32)
    large = jnp.minimum(large, nb - 1)
    return ret + jnp.where(n < max_exact, n, large)


def _bias_tables(rel_bias, tq):
    assert tq >= MAX_DISTANCE
    m = jnp.arange(2 * tq - 1, dtype=jnp.int32)
    rel = jnp.stack([d * tq + m - (tq - 1) for d in range(-BIAS_REACH, BIAS_REACH + 1)])
    w = jnp.transpose(rel_bias[_rel_bucket(rel)], (2, 0, 1)).astype(F32) * LOG2E
    period = 2 * tq
    v = jnp.concatenate([w[..., tq - 1::-1], jnp.zeros(w.shape[:2] + (1,), F32), w[..., :tq - 1:-1]], axis=-1)
    rep = jnp.tile(v, (1, 1, tq))[..., :tq * (period - 1)]
    return rep.reshape(w.shape[:2] + (tq, period - 1))[..., :tq]


def kernel(x_prompt, x_sample, norm1_g, w_in, b_gate, conv_w, conv_b, lru_wa_f, lru_ba_f, lru_wx_f, lru_bx_f, lru_lam_f, lru_wa_b, lru_ba_b, lru_wx_b, lru_bx_b, lru_lam_b, lam_q1, lam_k1, lam_q2, lam_k2, subln_g, rel_bias, w_rnn_out, w_attn_out, w_out, norm2_g, peer_wq, peer_keys, peer_u, peer_v, final_g):
    groups = [x_prompt.shape[:2], x_sample.shape[:2]]
    xs = (x_prompt.reshape(-1, D_MODEL), x_sample.reshape(-1, D_MODEL))
    t = xs[0].shape[0] + xs[1].shape[0]
    l = 0
    row = lambda a: a.reshape(1, -1)

    w_main = jnp.concatenate([w_in[l][:, :QKV_COL0], w_in[l][:, QKV_COL1:]], axis=1).astype(BF16)
    proj = _inproj(xs, row(norm1_g[l]), w_main)
    qk, vt = _qkv(xs, row(norm1_g[l]), w_in[l][:, QKV_COL0:QKV_COL1].astype(BF16))

    btiles = _bias_tables(rel_bias, 512)
    lamv = jnp.stack([lam_q1[l], lam_k1[l], lam_q2[l], lam_k2[l]])
    lru_f = (conv_w[l], row(conv_b[l]), lru_wa_f[l].astype(BF16), row(lru_ba_f[l]),
             lru_wx_f[l].astype(BF16), row(lru_bx_f[l]), row(lru_lam_f[l]))
    lru_b = (conv_w[l], row(conv_b[l]), lru_wa_b[l].astype(BF16), row(lru_ba_b[l]),
             lru_wx_b[l].astype(BF16), row(lru_bx_b[l]), row(lru_lam_b[l]))
    hf, hb, attn = [], [], []
    row0 = 0
    for nb, s in groups:
        hf.append(_lru(proj, row0, nb, s, *lru_f, reverse=False))
        hb.append(_lru(proj, row0, nb, s, *lru_b, reverse=True))
        attn.append(_attention(qk, vt, row0, nb, s, btiles, lamv, row(subln_g[l])))
        row0 += nb * s

    x1, xn2, qp = _merge(xs, hf, hb, proj, attn, b_gate[l].reshape(2, D_MODEL),
                         w_rnn_out[l].astype(BF16), w_attn_out[l].astype(BF16), w_out[l].astype(BF16),
                         row(norm2_g[l]), peer_wq[l].astype(BF16))

    idx, g = _router(qp, peer_keys[l].astype(BF16))
    idx_t = idx.T
    g_t = g.T
    tab_u = _pack_table(peer_u[l].reshape(N_EXPERTS * SUBLANES, LANES))
    tab_v = _pack_table(peer_v[l].reshape(N_EXPERTS * SUBLANES, LANES))
    w = _peer_u(idx_t, xn2.reshape(t, SUBLANES, LANES), g_t, tab_u)
    o = _peer_v(idx_t, w, tab_v).reshape(t, D_MODEL)

    y = _final(x1, o, row(final_g))
    n0 = groups[0][0] * groups[0][1]
    return (y[:n0].reshape(x_prompt.shape), y[n0:].reshape(x_sample.shape))
```
